```python
import math
import jax, jax.numpy as jnp
from jax import lax
import numpy as np

D_MODEL = 1024
BATCH = 4
SEQ = 4096
DEPTH = 2
DEC_BATCH = 128
DEC_SEQ = 8
PAST_LEN = 16384
PAGE_SIZE = 128

N_MIXERS = 2
N_POOL_LAYERS = (DEPTH + 1) // 2
N_ATTN_LAYERS = DEPTH // 2
POOL_WINDOWS = (2, 4, 8, 16)
N_POOL_GROUPS = len(POOL_WINDOWS)
POOL_GROUP_DIM = D_MODEL // N_POOL_GROUPS
POOL_STATE = max(POOL_WINDOWS) - 1
HEAD_DIM = 64
N_HEADS = D_MODEL // HEAD_DIM
N_KV_HEADS = 4
GQA_GROUP = N_HEADS // N_KV_HEADS
WINDOW = 128
ROPE_THETA = 10000.0
ATTN_SCALE = 1.0 / math.sqrt(HEAD_DIM)
PEER_HEADS = 8
PEER_NKEYS = 128
PEER_EXPERTS = PEER_NKEYS * PEER_NKEYS
PEER_DKEY = 256
PEER_TOPK = 16
PEER_BLOCK = 128

NORM_EPS = 1e-6
NEG_INF = -1e30

kernel_name = 'hybrid_pool_swa_peer_step'


def rmsnorm(x, g):
    xf = x.astype(jnp.float32)
    y = xf * lax.rsqrt(jnp.mean(xf * xf, axis=-1, keepdims=True) + NORM_EPS)
    return (y * g.astype(jnp.float32)).astype(x.dtype)


def rope(x, pos):
    half = HEAD_DIM // 2
    inv_freq = jnp.power(jnp.float32(ROPE_THETA), -jnp.arange(half, dtype=jnp.float32) * (2.0 / HEAD_DIM))
    ang = pos.astype(jnp.float32)[:, None] * inv_freq[None, :]
    cos = jnp.cos(ang)[None, :, None, :]
    sin = jnp.sin(ang)[None, :, None, :]
    xf = x.astype(jnp.float32)
    x1, x2 = xf[..., :half], xf[..., half:]
    return jnp.concatenate([x1 * cos - x2 * sin, x2 * cos + x1 * sin], axis=-1).astype(x.dtype)


def pool_mixer(h, prior, pos0, w_in, w_group, scale):
    B, S, _ = h.shape
    u = h @ w_in
    u_ext = jnp.concatenate([prior.astype(u.dtype), u], axis=1)
    uf = u_ext.astype(jnp.float32)
    cs = jnp.concatenate([jnp.zeros((B, 1, D_MODEL), jnp.float32), lax.cumsum(uf, axis=1)], axis=1)
    pos = pos0 + jnp.arange(S, dtype=jnp.int32)
    groups = []
    for g, w in enumerate(POOL_WINDOWS):
        c0, c1 = g * POOL_GROUP_DIM, (g + 1) * POOL_GROUP_DIM
        win_sum = (cs[:, POOL_STATE + 1:POOL_STATE + 1 + S, c0:c1]
                   - cs[:, POOL_STATE + 1 - w:POOL_STATE + 1 - w + S, c0:c1])
        count = jnp.minimum(pos + 1, w).astype(jnp.float32)[None, :, None]
        groups.append(win_sum / count - uf[:, POOL_STATE:, c0:c1])
    pooled = jnp.stack(groups, axis=2)
    y = jnp.einsum('bsgc,gcd->bsgd', pooled, w_group.astype(jnp.float32)).reshape(B, S, D_MODEL)
    y = y * scale.astype(jnp.float32)
    return y.astype(h.dtype), u_ext[:, -POOL_STATE:]


def attn_qkv(h, pos, w_qkv, q_gain, k_gain):
    B, S, _ = h.shape
    qkv = h @ w_qkv
    nq, nk = N_HEADS * HEAD_DIM, N_KV_HEADS * HEAD_DIM
    q = qkv[..., :nq].reshape(B, S, N_HEADS, HEAD_DIM)
    k = qkv[..., nq:nq + nk].reshape(B, S, N_KV_HEADS, HEAD_DIM)
    v = qkv[..., nq + nk:].reshape(B, S, N_KV_HEADS, HEAD_DIM)
    q = rope(rmsnorm(q, q_gain), pos)
    k = rope(rmsnorm(k, k_gain), pos)
    return q.reshape(B, S, N_KV_HEADS, GQA_GROUP, HEAD_DIM), k, v


def sink_softmax(s, mask, sinks):
    sk = sinks.astype(jnp.float32).reshape(N_KV_HEADS, GQA_GROUP, 1, 1)
    s = jnp.where(mask, s, NEG_INF)
    m = jnp.maximum(jnp.max(s, axis=-1, keepdims=True), sk)
    p = jnp.exp(s - m)
    return p / (jnp.sum(p, axis=-1, keepdims=True) + jnp.exp(sk - m))


def swa_prompt(q, k, v, sinks):
    B, S = q.shape[:2]
    nb = S // WINDOW
    qb = q.reshape(B, nb, WINDOW, N_KV_HEADS, GQA_GROUP, HEAD_DIM)
    pad = ((0, 0), (WINDOW, 0), (0, 0), (0, 0))
    kp = jnp.pad(k, pad).reshape(B, nb + 1, WINDOW, N_KV_HEADS, HEAD_DIM)
    vp = jnp.pad(v, pad).reshape(B, nb + 1, WINDOW, N_KV_HEADS, HEAD_DIM)
    kb = jnp.concatenate([kp[:, :-1], kp[:, 1:]], axis=2)
    vb = jnp.concatenate([vp[:, :-1], vp[:, 1:]], axis=2)
    s = jnp.einsum('bnqkgd,bnjkd->bnkgqj', qb, kb, preferred_element_type=jnp.float32) * ATTN_SCALE
    qi = jnp.arange(WINDOW)[:, None]
    kj = jnp.arange(2 * WINDOW)[None, :]
    blk = jnp.arange(nb)[:, None, None]
    mask = (kj > qi) & (kj <= qi + WINDOW) & (blk * WINDOW - WINDOW + kj >= 0)
    p = sink_softmax(s, mask[None, :, None, None], sinks)
    o = jnp.einsum('bnkgqj,bnjkd->bnqkgd', p.astype(v.dtype), vb)
    return o.reshape(B, S, N_HEADS * HEAD_DIM)


def swa_sample(q, k, v, ck, cv, sinks):
    B, T = q.shape[:2]
    kc = jnp.concatenate([ck.astype(k.dtype), k], axis=1)
    vc = jnp.concatenate([cv.astype(v.dtype), v], axis=1)
    s = jnp.einsum('btkgd,bjkd->bkgtj', q, kc, preferred_element_type=jnp.float32) * ATTN_SCALE
    qi = jnp.arange(T)[:, None]
    kj = jnp.arange(WINDOW + T)[None, :]
    mask = (kj > qi) & (kj <= qi + WINDOW) & (PAST_LEN - WINDOW + kj >= 0)
    p = sink_softmax(s, mask, sinks)
    o = jnp.einsum('bkgtj,bjkd->btkgd', p.astype(vc.dtype), vc)
    return o.reshape(B, T, N_HEADS * HEAD_DIM), kc[:, -WINDOW:], vc[:, -WINDOW:]


def peer_tokens(h, w_q, subkeys, u_tab, v_tab):
    n = h.shape[0]
    q = (h @ w_q).reshape(n, PEER_HEADS, 2, PEER_DKEY // 2)
    s = jnp.einsum('nhpc,hpkc->nhpk', q, subkeys, preferred_element_type=jnp.float32)
    top_s, top_i = lax.top_k(s, PEER_TOPK)
    cand = top_s[:, :, 0, :, None] + top_s[:, :, 1, None, :]
    best, flat = lax.top_k(cand.reshape(n, PEER_HEADS, PEER_TOPK * PEER_TOPK), PEER_TOPK)
    i1 = jnp.take_along_axis(top_i[:, :, 0], flat // PEER_TOPK, axis=-1)
    i2 = jnp.take_along_axis(top_i[:, :, 1], flat % PEER_TOPK, axis=-1)
    expert = i1 * PEER_NKEYS + i2
    gate = jax.nn.softmax(best, axis=-1)
    u = u_tab[expert]
    a = jnp.einsum('nd,nhkd->nhk', h, u, preferred_element_type=jnp.float32)
    w = gate * jax.nn.gelu(a, approximate=False)
    return jnp.einsum('nhk,nhkd->nd', w.astype(v_tab.dtype), v_tab[expert])


def peer_ffn(h, block, w_q, subkeys, u_tab, v_tab):
    shape = h.shape
    hb = h.reshape(-1, block, D_MODEL)
    y = lax.map(lambda t: peer_tokens(t, w_q, subkeys, u_tab, v_tab), hb)
    return y.reshape(shape)


def setup_inputs(seed: int = 0) -> dict:
    key = jax.random.key(seed)
    ks = jax.random.split(key, 20)
    f32 = jnp.float32

    def nrm(k, shape, scale):
        return jax.random.normal(k, shape, f32) * scale

    qkv_w = (N_HEADS + 2 * N_KV_HEADS) * HEAD_DIM
    return {
        'x_prompt': nrm(ks[0], (BATCH, SEQ, D_MODEL), 1.0),
        'x_sample': nrm(ks[1], (DEC_BATCH, DEC_SEQ, D_MODEL), 1.0),
        'state_pool': nrm(ks[2], (N_POOL_LAYERS, DEC_BATCH, POOL_STATE, D_MODEL), 1.0),
        'cache_k': nrm(ks[3], (N_ATTN_LAYERS, DEC_BATCH, WINDOW, N_KV_HEADS, HEAD_DIM), 1.0),
        'cache_v': nrm(ks[4], (N_ATTN_LAYERS, DEC_BATCH, WINDOW, N_KV_HEADS, HEAD_DIM), 1.0),
        'norm_mix': 1.0 + nrm(ks[5], (DEPTH, D_MODEL), 0.05),
        'norm_ffn': 1.0 + nrm(ks[6], (DEPTH, D_MODEL), 0.05),
        'pool_w_in': nrm(ks[7], (N_POOL_LAYERS, D_MODEL, D_MODEL), D_MODEL ** -0.5),
        'pool_w_group': nrm(ks[8], (N_POOL_LAYERS, N_POOL_GROUPS, POOL_GROUP_DIM, POOL_GROUP_DIM), POOL_GROUP_DIM ** -0.5),
        'pool_scale': 1.0 + nrm(ks[9], (N_POOL_LAYERS, D_MODEL), 0.05),
        'attn_w_qkv': nrm(ks[10], (N_ATTN_LAYERS, D_MODEL, qkv_w), D_MODEL ** -0.5),
        'attn_q_norm': 1.0 + nrm(ks[11], (N_ATTN_LAYERS, HEAD_DIM), 0.05),
        'attn_k_norm': 1.0 + nrm(ks[12], (N_ATTN_LAYERS, HEAD_DIM), 0.05),
        'attn_sinks': nrm(ks[13], (N_ATTN_LAYERS, N_HEADS), 0.5),
        'attn_w_o': nrm(ks[14], (N_ATTN_LAYERS, N_HEADS * HEAD_DIM, D_MODEL), (N_HEADS * HEAD_DIM) ** -0.5),
        'peer_w_q': nrm(ks[15], (DEPTH, D_MODEL, PEER_HEADS * PEER_DKEY), D_MODEL ** -0.5),
        'peer_subkeys': nrm(ks[16], (DEPTH, PEER_HEADS, 2, PEER_NKEYS, PEER_DKEY // 2), (PEER_DKEY // 2) ** -0.5),
        'peer_u': nrm(ks[17], (DEPTH, PEER_EXPERTS, D_MODEL), D_MODEL ** -0.5),
        'peer_v': nrm(ks[18], (DEPTH, PEER_EXPERTS, D_MODEL), PEER_HEADS ** -0.5),
    }


def reference(x_prompt, x_sample, state_pool, cache_k, cache_v, norm_mix, norm_ffn,
              pool_w_in, pool_w_group, pool_scale, attn_w_qkv, attn_q_norm, attn_k_norm,
              attn_sinks, attn_w_o, peer_w_q, peer_subkeys, peer_u, peer_v):
    xp, xs = x_prompt, x_sample
    pos_p = jnp.arange(SEQ, dtype=jnp.int32)
    pos_s = PAST_LEN + jnp.arange(DEC_SEQ, dtype=jnp.int32)
    pool_p, pool_s, kp_l, vp_l, ks_l, vs_l = [], [], [], [], [], []
    for i in range(DEPTH):
        j = i // N_MIXERS
        hp = rmsnorm(xp, norm_mix[i])
        hs = rmsnorm(xs, norm_mix[i])
        if i % N_MIXERS == 0:
            prior_p = jnp.zeros((hp.shape[0], POOL_STATE, D_MODEL), hp.dtype)
            yp, st_p = pool_mixer(hp, prior_p, 0, pool_w_in[j], pool_w_group[j], pool_scale[j])
            ys, st_s = pool_mixer(hs, state_pool[j], PAST_LEN, pool_w_in[j], pool_w_group[j], pool_scale[j])
            pool_p.append(st_p)
            pool_s.append(st_s)
        else:
            qp, kp, vp = attn_qkv(hp, pos_p, attn_w_qkv[j], attn_q_norm[j], attn_k_norm[j])
            yp = swa_prompt(qp, kp, vp, attn_sinks[j]) @ attn_w_o[j]
            qs, kq, vq = attn_qkv(hs, pos_s, attn_w_qkv[j], attn_q_norm[j], attn_k_norm[j])
            os_, nk_s, nv_s = swa_sample(qs, kq, vq, cache_k[j], cache_v[j], attn_sinks[j])
            ys = os_ @ attn_w_o[j]
            kp_l.append(kp[:, -WINDOW:])
            vp_l.append(vp[:, -WINDOW:])
            ks_l.append(nk_s)
            vs_l.append(nv_s)
        xp = xp + yp
        xs = xs + ys
        xp = xp + peer_ffn(rmsnorm(xp, norm_ffn[i]), PEER_BLOCK, peer_w_q[i], peer_subkeys[i], peer_u[i], peer_v[i])
        xs = xs + peer_ffn(rmsnorm(xs, norm_ffn[i]), DEC_SEQ, peer_w_q[i], peer_subkeys[i], peer_u[i], peer_v[i])
    return (xp, xs, jnp.stack(pool_p), jnp.stack(pool_s), jnp.stack(kp_l), jnp.stack(vp_l), jnp.stack(ks_l), jnp.stack(vs_l))
```

```python
import functools
import math

import jax
import jax.numpy as jnp
import numpy as np
from jax import lax
from jax.experimental import pallas as pl
from jax.experimental.pallas import tpu as pltpu

F32 = jnp.float32
BF16 = jnp.bfloat16

D_MODEL = 1024
LANES = 128
SUBLANES = 8
NORM_EPS = 1e-6
NEG_INF = float("-inf")

POOL_WINDOWS = (2, 4, 8, 16)
POOL_GROUP_DIM = D_MODEL // len(POOL_WINDOWS)
POOL_HIST = 16

HEAD_DIM = 64
N_HEADS = 16
N_KV_HEADS = 4
GQA_GROUP = N_HEADS // N_KV_HEADS
WINDOW = 128
ROPE_THETA = 10000.0
ATTN_SCALE = 1.0 / math.sqrt(HEAD_DIM)
Q_WIDTH = N_HEADS * HEAD_DIM
KV_WIDTH = N_KV_HEADS * HEAD_DIM
MASK_NEG = -1e30

PEER_HEADS = 8
PEER_NKEYS = 128
PEER_TOPK = 16
PEER_DKEY = 256

VMEM_LIMIT = 56 * 1024 * 1024


def _cparams(n_axes):
    return pltpu.CompilerParams(dimension_semantics=("arbitrary",) * n_axes,
                                vmem_limit_bytes=VMEM_LIMIT)


def _rmsnorm(x, g):
    return x * lax.rsqrt(jnp.mean(x * x, axis=-1, keepdims=True) + NORM_EPS) * g


def _dot(a, b):
    return jnp.dot(a, b, preferred_element_type=F32)


def _dot_nt(a, b):
    return lax.dot_general(a, b, (((1,), (1,)), ((), ())), preferred_element_type=F32)


def _dot_tn(a, b):
    return lax.dot_general(a, b, (((0,), (0,)), ((), ())), preferred_element_type=F32)


def _pool_prompt_kernel(x_ref, g_ref, win_ref, wg_ref, sc_ref, y_ref, st_ref, ext_s, *, ts):
    sb = pl.program_id(1)

    @pl.when(sb == 0)
    def _():
        ext_s[0:POOL_HIST, :] = jnp.zeros((POOL_HIST, D_MODEL), F32)

    x = x_ref[0]
    h = _rmsnorm(x, g_ref[...])
    u = _dot(h.astype(BF16), win_ref[...])
    ext_s[POOL_HIST:POOL_HIST + ts, :] = u
    pos = sb * ts + lax.broadcasted_iota(jnp.int32, (ts, 1), 0)
    outs = []
    for g, w in enumerate(POOL_WINDOWS):
        c0 = g * POOL_GROUP_DIM
        ug = u[:, c0:c0 + POOL_GROUP_DIM]
        win_sum = ug
        for j in range(1, w):
            win_sum = win_sum + ext_s[POOL_HIST - j:POOL_HIST - j + ts, c0:c0 + POOL_GROUP_DIM]
        count = jnp.minimum(pos + 1, w).astype(F32)
        pooled = win_sum / count - ug
        outs.append(_dot(pooled.astype(BF16), wg_ref[g]))
    y = jnp.concatenate(outs, axis=1) * sc_ref[...]
    y_ref[0] = x + y
    st_ref[0] = u[ts - POOL_HIST:ts, :]
    ext_s[0:POOL_HIST, :] = u[ts - POOL_HIST:ts, :]


def _pool_prompt(x, g, win, wg, sc, ts=512):
    b, s, _ = x.shape
    return pl.pallas_call(
        functools.partial(_pool_prompt_kernel, ts=ts),
        grid=(b, s // ts),
        in_specs=[
            pl.BlockSpec((1, ts, D_MODEL), lambda i, j: (i, j, 0)),
            pl.BlockSpec((1, D_MODEL), lambda i, j: (0, 0)),
            pl.BlockSpec((D_MODEL, D_MODEL), lambda i, j: (0, 0)),
            pl.BlockSpec((len(POOL_WINDOWS), POOL_GROUP_DIM, POOL_GROUP_DIM), lambda i, j: (0, 0, 0)),
            pl.BlockSpec((1, D_MODEL), lambda i, j: (0, 0)),
        ],
        out_specs=[
            pl.BlockSpec((1, ts, D_MODEL), lambda i, j: (i, j, 0)),
            pl.BlockSpec((1, POOL_HIST, D_MODEL), lambda i, j: (i, 0, 0)),
        ],
        out_shape=[
            jax.ShapeDtypeStruct((b, s, D_MODEL), F32),
            jax.ShapeDtypeStruct((b, POOL_HIST, D_MODEL), F32),
        ],
        scratch_shapes=[pltpu.VMEM((POOL_HIST + ts, D_MODEL), F32)],
        compiler_params=_cparams(2),
        name="pool_prompt",
    )(x, g, win, wg, sc)


def _pool_sample_kernel(x_ref, pr_ref, g_ref, win_ref, wg_ref, sc_ref, y_ref, st_ref, ext_s, *, bb, t):
    x = x_ref[...]
    h = _rmsnorm(x, g_ref[...])
    u = _dot(h.astype(BF16), win_ref[...])
    ext_s[:, 0:POOL_HIST, :] = pr_ref[...]
    ext_s[:, POOL_HIST:POOL_HIST + t, :] = u.reshape(bb, t, D_MODEL)
    outs = []
    for g, w in enumerate(POOL_WINDOWS):
        c0 = g * POOL_GROUP_DIM
        win_sum = ext_s[:, POOL_HIST:POOL_HIST + t, c0:c0 + POOL_GROUP_DIM]
        for j in range(1, w):
            win_sum = win_sum + ext_s[:, POOL_HIST - j:POOL_HIST - j + t, c0:c0 + POOL_GROUP_DIM]
        pooled = win_sum.reshape(bb * t, POOL_GROUP_DIM) / float(w) - u[:, c0:c0 + POOL_GROUP_DIM]
        outs.append(_dot(pooled.astype(BF16), wg_ref[g]))
    y = jnp.concatenate(outs, axis=1) * sc_ref[...]
    y_ref[...] = x + y
    st_ref[...] = ext_s[:, t:t + POOL_HIST, :]


def _pool_sample(x2, prior16, g, win, wg, sc, t, bb=32):
    n = x2.shape[0]
    nb = n // t
    return pl.pallas_call(
        functools.partial(_pool_sample_kernel, bb=bb, t=t),
        grid=(nb // bb,),
        in_specs=[
            pl.BlockSpec((bb * t, D_MODEL), lambda i: (i, 0)),
            pl.BlockSpec((bb, POOL_HIST, D_MODEL), lambda i: (i, 0, 0)),
            pl.BlockSpec((1, D_MODEL), lambda i: (0, 0)),
            pl.BlockSpec((D_MODEL, D_MODEL), lambda i: (0, 0)),
            pl.BlockSpec((len(POOL_WINDOWS), POOL_GROUP_DIM, POOL_GROUP_DIM), lambda i: (0, 0, 0)),
            pl.BlockSpec((1, D_MODEL), lambda i: (0, 0)),
        ],
        out_specs=[
            pl.BlockSpec((bb * t, D_MODEL), lambda i: (i, 0)),
            pl.BlockSpec((bb, POOL_HIST, D_MODEL), lambda i: (i, 0, 0)),
        ],
        out_shape=[
            jax.ShapeDtypeStruct((n, D_MODEL), F32),
            jax.ShapeDtypeStruct((nb, POOL_HIST, D_MODEL), F32),
        ],
        scratch_shapes=[pltpu.VMEM((bb, POOL_HIST + t, D_MODEL), F32)],
        compiler_params=_cparams(1),
        name="pool_sample",
    )(x2, prior16, g, win, wg, sc)


def _qkv_kernel(x_ref, g_ref, w_ref, seg_ref, gain_ref, cos_ref, sin_ref, q_ref, k_ref, v_ref):
    x = x_ref[...]
    h = _rmsnorm(x, g_ref[...])
    qkv = _dot(h.astype(BF16), w_ref[...])
    nqk = Q_WIDTH + KV_WIDTH
    qk = qkv[:, :nqk]
    sq = qk * qk
    sq_hi = sq.astype(BF16)
    sq_lo = (sq - sq_hi.astype(F32)).astype(BF16)
    seg = seg_ref[...]
    cw = seg.shape[0]
    parts = []
    for c in range(nqk // cw):
        sl = slice(c * cw, (c + 1) * cw)
        parts.append(_dot(sq_hi[:, sl], seg) + _dot(sq_lo[:, sl], seg))
    ssq = jnp.concatenate(parts, axis=1)
    qk = qk * lax.rsqrt(ssq * (1.0 / HEAD_DIM) + NORM_EPS) * gain_ref[...]
    half = HEAD_DIM // 2
    lane = lax.broadcasted_iota(jnp.int32, qk.shape, 1)
    first = (lane % HEAD_DIM) < half
    up = pltpu.roll(qk, nqk - half, 1)
    down = pltpu.roll(qk, half, 1)
    rot = jnp.where(first, -up, down)
    reps = nqk // cos_ref.shape[1]
    cos = jnp.concatenate([cos_ref[...]] * reps, axis=1)
    sin = jnp.concatenate([sin_ref[...]] * reps, axis=1)
    qk = qk * cos + rot * sin
    q_ref[...] = qk[:, :Q_WIDTH]
    k_ref[...] = qk[:, Q_WIDTH:]
    v_ref[...] = qkv[:, nqk:]


def _qkv(x2, g, w, seg, gain, cos, sin, tq, pos_blocks):
    n = x2.shape[0]
    tw = cos.shape[1]
    return pl.pallas_call(
        _qkv_kernel,
        grid=(n // tq,),
        in_specs=[
            pl.BlockSpec((tq, D_MODEL), lambda i: (i, 0)),
            pl.BlockSpec((1, D_MODEL), lambda i: (0, 0)),
            pl.BlockSpec(w.shape, lambda i: (0, 0)),
            pl.BlockSpec(seg.shape, lambda i: (0, 0)),
            pl.BlockSpec((1, Q_WIDTH + KV_WIDTH), lambda i: (0, 0)),
            pl.BlockSpec((tq, tw), lambda i: (i % pos_blocks, 0)),
            pl.BlockSpec((tq, tw), lambda i: (i % pos_blocks, 0)),
        ],
        out_specs=[
            pl.BlockSpec((tq, Q_WIDTH), lambda i: (i, 0)),
            pl.BlockSpec((tq, KV_WIDTH), lambda i: (i, 0)),
            pl.BlockSpec((tq, KV_WIDTH), lambda i: (i, 0)),
        ],
        out_shape=[
            jax.ShapeDtypeStruct((n, Q_WIDTH), F32),
            jax.ShapeDtypeStruct((n, KV_WIDTH), F32),
            jax.ShapeDtypeStruct((n, KV_WIDTH), F32),
        ],
        compiler_params=_cparams(1),
        name="attn_qkv",
    )(x2, g, w, seg, gain, cos, sin)


def _attn_prompt_kernel(x_ref, q_ref, kp_ref, kc_ref, vp_ref, vc_ref, sink_ref, wo_ref, y_ref):
    nb = pl.program_id(1)
    q = q_ref[0].astype(BF16)
    k2 = jnp.concatenate([kp_ref[0], kc_ref[0]], axis=0).astype(BF16)
    v2 = jnp.concatenate([vp_ref[0], vc_ref[0]], axis=0).astype(BF16)
    rows = GQA_GROUP * WINDOW
    qi = lax.broadcasted_iota(jnp.int32, (rows, 2 * WINDOW), 0) % WINDOW
    kj = lax.broadcasted_iota(jnp.int32, (rows, 2 * WINDOW), 1)
    mask = (kj > qi) & (kj <= qi + WINDOW) & (nb * WINDOW - WINDOW + kj >= 0)
    outs = []
    for kh in range(N_KV_HEADS):
        kk = k2[:, kh * HEAD_DIM:(kh + 1) * HEAD_DIM]
        vv = v2[:, kh * HEAD_DIM:(kh + 1) * HEAD_DIM]
        q4 = jnp.concatenate(
            [q[:, (kh * GQA_GROUP + g) * HEAD_DIM:(kh * GQA_GROUP + g + 1) * HEAD_DIM] for g in range(GQA_GROUP)],
            axis=0)
        sk = jnp.concatenate(
            [jnp.full((WINDOW, 1), 1.0, F32) * sink_ref[kh * GQA_GROUP + g] for g in range(GQA_GROUP)], axis=0)
        s = _dot_nt(q4, kk) * ATTN_SCALE
        s = jnp.where(mask, s, MASK_NEG)
        m = jnp.maximum(jnp.max(s, axis=-1, keepdims=True), sk)
        p = jnp.exp(s - m)
        p = p / (jnp.sum(p, axis=-1, keepdims=True) + jnp.exp(sk - m))
        o4 = _dot(p.astype(BF16), vv)
        for g in range(GQA_GROUP):
            outs.append(o4[g * WINDOW:(g + 1) * WINDOW, :])
    o = jnp.concatenate(outs, axis=1)
    y_ref[0] = x_ref[0] + _dot(o.astype(BF16), wo_ref[...])


def _attn_prompt(x, q, k, v, sinks, wo):
    b, s, _ = x.shape
    nblk = s // WINDOW
    cur = lambda i, j: (i, j, 0)
    prev = lambda i, j: (i, jnp.maximum(j - 1, 0), 0)
    return pl.pallas_call(
        _attn_prompt_kernel,
        grid=(b, nblk),
        in_specs=[
            pl.BlockSpec((1, WINDOW, D_MODEL), cur),
            pl.BlockSpec((1, WINDOW, Q_WIDTH), cur),
            pl.BlockSpec((1, WINDOW, KV_WIDTH), prev),
            pl.BlockSpec((1, WINDOW, KV_WIDTH), cur),
            pl.BlockSpec((1, WINDOW, KV_WIDTH), prev),
            pl.BlockSpec((1, WINDOW, KV_WIDTH), cur),
            pl.BlockSpec(memory_space=pltpu.SMEM),
            pl.BlockSpec((Q_WIDTH, D_MODEL), lambda i, j: (0, 0)),
        ],
        out_specs=pl.BlockSpec((1, WINDOW, D_MODEL), cur),
        out_shape=jax.ShapeDtypeStruct((b, s, D_MODEL), F32),
        compiler_params=_cparams(2),
        name="attn_prompt",
    )(x, q, k, k, v, v, sinks, wo)


def _attn_sample_kernel(x_ref, q_ref, kn_ref, vn_ref, ck_ref, cv_ref, sink_ref, wo_ref,
                        y_ref, nk_ref, nv_ref, *, bb, t):
    q = q_ref[...].reshape(bb, t, Q_WIDTH)
    kn = kn_ref[...].reshape(bb, t, KV_WIDTH)
    vn = vn_ref[...].reshape(bb, t, KV_WIDTH)
    ck = ck_ref[...]
    cv = cv_ref[...]
    nk_ref[:, 0:WINDOW - t, :] = ck[:, t:, :]
    nk_ref[:, WINDOW - t:, :] = kn
    nv_ref[:, 0:WINDOW - t, :] = cv[:, t:, :]
    nv_ref[:, WINDOW - t:, :] = vn
    rows = GQA_GROUP * t
    qi_c = lax.broadcasted_iota(jnp.int32, (bb, rows, WINDOW), 1) % t
    kj_c = lax.broadcasted_iota(jnp.int32, (bb, rows, WINDOW), 2)
    mask_c = kj_c > qi_c
    qi_n = lax.broadcasted_iota(jnp.int32, (bb, rows, t), 1) % t
    kj_n = lax.broadcasted_iota(jnp.int32, (bb, rows, t), 2)
    mask_n = kj_n <= qi_n
    ckb, cvb, knb, vnb = ck.astype(BF16), cv.astype(BF16), kn.astype(BF16), vn.astype(BF16)
    outs = []
    for kh in range(N_KV_HEADS):
        hs = slice(kh * HEAD_DIM, (kh + 1) * HEAD_DIM)
        q4 = jnp.concatenate(
            [q[:, :, (kh * GQA_GROUP + g) * HEAD_DIM:(kh * GQA_GROUP + g + 1) * HEAD_DIM] for g in range(GQA_GROUP)],
            axis=1).astype(BF16)
        sk = jnp.concatenate(
            [jnp.full((1, t, 1), 1.0, F32) * sink_ref[kh * GQA_GROUP + g] for g in range(GQA_GROUP)], axis=1)
        s_c = jnp.einsum("bqd,bkd->bqk", q4, ckb[:, :, hs], preferred_element_type=F32) * ATTN_SCALE
        s_n = jnp.einsum("bqd,bkd->bqk", q4, knb[:, :, hs], preferred_element_type=F32) * ATTN_SCALE
        s_c = jnp.where(mask_c, s_c, MASK_NEG)
        s_n = jnp.where(mask_n, s_n, MASK_NEG)
        m = jnp.maximum(jnp.maximum(jnp.max(s_c, axis=-1, keepdims=True),
                                    jnp.max(s_n, axis=-1, keepdims=True)), sk)
        p_c = jnp.exp(s_c - m)
        p_n = jnp.exp(s_n - m)
        den = jnp.sum(p_c, axis=-1, keepdims=True) + jnp.sum(p_n, axis=-1, keepdims=True) + jnp.exp(sk - m)
        inv = 1.0 / den
        o4 = (jnp.einsum("bqk,bkd->bqd", (p_c * inv).astype(BF16), cvb[:, :, hs], preferred_element_type=F32)
              + jnp.einsum("bqk,bkd->bqd", (p_n * inv).astype(BF16), vnb[:, :, hs], preferred_element_type=F32))
        for g in range(GQA_GROUP):
            outs.append(o4[:, g * t:(g + 1) * t, :])
    o = jnp.concatenate(outs, axis=2).reshape(bb * t, Q_WIDTH)
    y_ref[...] = x_ref[...] + _dot(o.astype(BF16), wo_ref[...])


def _attn_sample(x2, q, kn, vn, ck, cv, sinks, wo, t, bb=16):
    n = x2.shape[0]
    nb = n // t
    row = lambda i: (i, 0)
    cache = lambda i: (i, 0, 0)
    return pl.pallas_call(
        functools.partial(_attn_sample_kernel, bb=bb, t=t),
        grid=(nb // bb,),
        in_specs=[
            pl.BlockSpec((bb * t, D_MODEL), row),
            pl.BlockSpec((bb * t, Q_WIDTH), row),
            pl.BlockSpec((bb * t, KV_WIDTH), row),
            pl.BlockSpec((bb * t, KV_WIDTH), row),
            pl.BlockSpec((bb, WINDOW, KV_WIDTH), cache),
            pl.BlockSpec((bb, WINDOW, KV_WIDTH), cache),
            pl.BlockSpec(memory_space=pltpu.SMEM),
            pl.BlockSpec((Q_WIDTH, D_MODEL), lambda i: (0, 0)),
        ],
        out_specs=[
            pl.BlockSpec((bb * t, D_MODEL), row),
            pl.BlockSpec((bb, WINDOW, KV_WIDTH), cache),
            pl.BlockSpec((bb, WINDOW, KV_WIDTH), cache),
        ],
        out_shape=[
            jax.ShapeDtypeStruct((n, D_MODEL), F32),
            jax.ShapeDtypeStruct((nb, WINDOW, KV_WIDTH), F32),
            jax.ShapeDtypeStruct((nb, WINDOW, KV_WIDTH), F32),
        ],
        compiler_params=_cparams(1),
        name="attn_sample",
    )(x2, q, kn, vn, ck, cv, sinks, wo)


_CAND_LIMITS = tuple(PEER_TOPK // (r2 + 1) for r2 in range(1, SUBLANES))
_CAND_ROWS = PEER_TOPK + SUBLANES * len(_CAND_LIMITS) + SUBLANES


def _peer_select(s1, s2, vt_s, cp_s):
    def top_values(s, slot):
        work = s
        for r in range(PEER_TOPK):
            m = jnp.max(work, axis=0, keepdims=True)
            vt_s[slot, r:r + 1, :] = m
            work = jnp.where(work >= m, NEG_INF, work)

    top_values(s1, 0)
    top_values(s2, 1)
    v1 = vt_s[0]
    v2 = vt_s[1]
    row8 = lax.broadcasted_iota(jnp.int32, (SUBLANES, LANES), 0)
    groups = [v1 + v2[0:1]]
    for r2, lim in enumerate(_CAND_LIMITS, start=1):
        g = v1[0:SUBLANES] + v2[r2:r2 + 1]
        if lim < SUBLANES:
            g = jnp.where(row8 < lim, g, NEG_INF)
        groups.append(g)
    groups.append(v2[SUBLANES:] + v1[0:1])
    cand = jnp.concatenate(groups, axis=0)
    work = cand
    tau = None
    for r in range(PEER_TOPK):
        tau = jnp.max(work, axis=0, keepdims=True)
        if r + 1 < PEER_TOPK:
            work = jnp.where(work >= tau, NEG_INF, work)
    best = v1[0:1] + v2[0:1]
    sel = cand >= tau
    z = jnp.sum(jnp.where(sel, jnp.exp(cand - best), 0.0), axis=0, keepdims=True)
    self32 = sel.astype(F32)
    lo = self32[0:SUBLANES]
    for gi in range(len(_CAND_LIMITS)):
        a = PEER_TOPK + SUBLANES * gi
        lo = lo + self32[a:a + SUBLANES]
    tail = jnp.sum(self32[_CAND_ROWS - SUBLANES:], axis=0, keepdims=True)
    lo = lo + jnp.where(row8 == 0, tail, 0.0)
    cp_s[0:SUBLANES, :] = lo
    cp_s[SUBLANES:, :] = self32[SUBLANES:PEER_TOPK]
    count1 = jnp.zeros_like(s1)
    rank2 = jnp.zeros_like(s2)
    for r in range(PEER_TOPK):
        count1 = jnp.maximum(count1, jnp.where(s1 >= vt_s[0, r:r + 1, :], cp_s[r:r + 1, :], 0.0))
        rank2 = rank2 + jnp.where(vt_s[1, r:r + 1, :] > s2, 1.0, 0.0)
    e2 = jnp.exp(s2 - v2[0:1])
    e1z = jnp.exp(s1 - v1[0:1]) * (1.0 / z)
    return rank2, e2, count1, e1z


def _peer_kernel(x_ref, g_ref, wq_ref, sk_ref, u_ref, v_ref, o_ref,
                 hn_s, s_s, vt_s, cp_s, r2_s, e2_s, c1_s, e1_s, a_s, w_s, y_s, *, tm, te):
    j = pl.program_id(1)
    nj = pl.num_programs(1)
    nchunk = tm // LANES
    gi = te // PEER_NKEYS

    @pl.when(j == 0)
    def _():
        hn = _rmsnorm(x_ref[...], g_ref[...]).astype(BF16)
        hn_s[...] = hn
        for h in range(PEER_HEADS):
            qh = _dot(hn, wq_ref[:, h * PEER_DKEY:(h + 1) * PEER_DKEY]).astype(BF16)
            for p in range(2):
                half = PEER_DKEY // 2
                s_s[2 * h + p] = _dot_nt(sk_ref[h, p], qh[:, p * half:(p + 1) * half])

        def select(idx, carry):
            h = idx // nchunk
            col = pl.multiple_of((idx % nchunk) * LANES, LANES)
            cs = pl.ds(col, LANES)
            rank2, e2, count1, e1z = _peer_select(s_s[2 * h, :, cs], s_s[2 * h + 1, :, cs], vt_s, cp_s)
            r2_s[h, :, cs] = rank2
            e2_s[h, :, cs] = e2
            c1_s[h, :, cs] = count1
            e1_s[h, :, cs] = e1z
            return carry

        lax.fori_loop(0, PEER_HEADS * nchunk, select, 0)
        y_s[...] = jnp.zeros_like(y_s)

    a_s[...] = _dot_nt(u_ref[...], hn_s[...])

    def gate(idx, carry):
        grp = idx // nchunk
        col = pl.multiple_of((idx % nchunk) * LANES, LANES)
        cs = pl.ds(col, LANES)
        i1s = pl.ds(pl.multiple_of(j * gi + grp * SUBLANES, SUBLANES), SUBLANES)
        cnt = [c1_s[h, i1s, cs] for h in range(PEER_HEADS)]
        e1 = [e1_s[h, i1s, cs] for h in range(PEER_HEADS)]
        for l in range(SUBLANES):
            acc = jnp.zeros((PEER_NKEYS, LANES), F32)
            for h in range(PEER_HEADS):
                acc = acc + jnp.where(r2_s[h, :, cs] < cnt[h][l:l + 1], e2_s[h, :, cs], 0.0) * e1[h][l:l + 1]
            rs = pl.ds(pl.multiple_of((grp * SUBLANES + l) * PEER_NKEYS, PEER_NKEYS), PEER_NKEYS)
            a = a_s[rs, cs]
            gelu = 0.5 * a * (1.0 + lax.erf(a * math.sqrt(0.5)))
            w_s[rs, cs] = (acc * gelu).astype(BF16)
        return carry

    lax.fori_loop(0, (gi // SUBLANES) * nchunk, gate, 0)
    y_s[...] += _dot_tn(w_s[...], v_ref[...])

    @pl.when(j == nj - 1)
    def _():
        o_ref[...] = x_ref[...] + y_s[...]


def _peer(x2, g, wq, sk, u_tab, v_tab, tm=512, te=1024):
    n = x2.shape[0]
    n_exp = u_tab.shape[0]
    return pl.pallas_call(
        functools.partial(_peer_kernel, tm=tm, te=te),
        grid=(n // tm, n_exp // te),
        in_specs=[
            pl.BlockSpec((tm, D_MODEL), lambda i, j: (i, 0)),
            pl.BlockSpec((1, D_MODEL), lambda i, j: (0, 0)),
            pl.BlockSpec(wq.shape, lambda i, j: (0, 0)),
            pl.BlockSpec(sk.shape, lambda i, j: (0, 0, 0, 0)),
            pl.BlockSpec((te, D_MODEL), lambda i, j: (j, 0)),
            pl.BlockSpec((te, D_MODEL), lambda i, j: (j, 0)),
        ],
        out_specs=pl.BlockSpec((tm, D_MODEL), lambda i, j: (i, 0)),
        out_shape=jax.ShapeDtypeStruct((n, D_MODEL), F32),
        scratch_shapes=[
            pltpu.VMEM((tm, D_MODEL), BF16),
            pltpu.VMEM((2 * PEER_HEADS, PEER_NKEYS, tm), F32),
            pltpu.VMEM((2, PEER_TOPK, LANES), F32),
            pltpu.VMEM((PEER_TOPK, LANES), F32),
            pltpu.VMEM((PEER_HEADS, PEER_NKEYS, tm), F32),
            pltpu.VMEM((PEER_HEADS, PEER_NKEYS, tm), F32),
            pltpu.VMEM((PEER_HEADS, PEER_NKEYS, tm), F32),
            pltpu.VMEM((PEER_HEADS, PEER_NKEYS, tm), F32),
            pltpu.VMEM((te, tm), F32),
            pltpu.VMEM((te, tm), BF16),
            pltpu.VMEM((tm, D_MODEL), F32),
        ],
        compiler_params=_cparams(2),
        name="peer",
    )(x2, g, wq, sk, u_tab, v_tab)


def _rope_tables(pos):
    half = HEAD_DIM // 2
    inv_freq = jnp.power(jnp.float32(ROPE_THETA), -jnp.arange(half, dtype=F32) * (2.0 / HEAD_DIM))
    ang = pos.astype(F32)[:, None] * inv_freq[None, :]
    cos = jnp.concatenate([jnp.cos(ang)] * 4, axis=1)
    sin = jnp.concatenate([jnp.sin(ang)] * 4, axis=1)
    return cos, sin


def kernel(x_prompt, x_sample, state_pool, cache_k, cache_v, norm_mix, norm_ffn, pool_w_in, pool_w_group,
           pool_scale, attn_w_qkv, attn_q_norm, attn_k_norm, attn_sinks, attn_w_o, peer_w_q, peer_subkeys,
           peer_u, peer_v):
    b, s, _ = x_prompt.shape
    db, t, _ = x_sample.shape
    past_len = 16384
    depth = norm_mix.shape[0]
    xp = x_prompt
    xs = x_sample.reshape(db * t, D_MODEL)
    seg = jnp.asarray(np.kron(np.eye(256 // HEAD_DIM), np.ones((HEAD_DIM, HEAD_DIM))), BF16)
    cos_p, sin_p = _rope_tables(jnp.arange(s, dtype=jnp.int32))
    cos_s, sin_s = _rope_tables(past_len + jnp.arange(t, dtype=jnp.int32))
    pool_p, pool_s, kp_l, vp_l, ks_l, vs_l = [], [], [], [], [], []
    for i in range(depth):
        jl = i // 2
        g_mix = norm_mix[i][None, :]
        if i % 2 == 0:
            win = pool_w_in[jl].astype(BF16)
            wg = pool_w_group[jl].astype(BF16)
            sc = pool_scale[jl][None, :]
            xp, st_p = _pool_prompt(xp, g_mix, win, wg, sc)
            prior16 = jnp.pad(state_pool[jl], ((0, 0), (1, 0), (0, 0)))
            xs, st_s = _pool_sample(xs, prior16, g_mix, win, wg, sc, t)
            pool_p.append(st_p[:, 1:])
            pool_s.append(st_s[:, 1:])
        else:
            wqkv = attn_w_qkv[jl].astype(BF16)
            wo = attn_w_o[jl].astype(BF16)
            gain = jnp.concatenate([jnp.tile(attn_q_norm[jl], N_HEADS), jnp.tile(attn_k_norm[jl], N_KV_HEADS)])[None, :]
            sinks = attn_sinks[jl]
            tq = 512
            qp, kp, vp = _qkv(xp.reshape(b * s, D_MODEL), g_mix, wqkv, seg, gain, cos_p, sin_p, tq, s // tq)
            kp3 = kp.reshape(b, s, KV_WIDTH)
            vp3 = vp.reshape(b, s, KV_WIDTH)
            xp = _attn_prompt(xp, qp.reshape(b, s, Q_WIDTH), kp3, vp3, sinks, wo)
            cos_st = jnp.tile(cos_s, (tq // t, 1))
            sin_st = jnp.tile(sin_s, (tq // t, 1))
            qs, kq, vq = _qkv(xs, g_mix, wqkv, seg, gain, cos_st, sin_st, tq, 1)
            ck = cache_k[jl].reshape(db, WINDOW, KV_WIDTH)
            cv = cache_v[jl].reshape(db, WINDOW, KV_WIDTH)
            xs, nk_s, nv_s = _attn_sample(xs, qs, kq, vq, ck, cv, sinks, wo, t)
            kp_l.append(kp3[:, -WINDOW:].reshape(b, WINDOW, N_KV_HEADS, HEAD_DIM))
            vp_l.append(vp3[:, -WINDOW:].reshape(b, WINDOW, N_KV_HEADS, HEAD_DIM))
            ks_l.append(nk_s.reshape(db, WINDOW, N_KV_HEADS, HEAD_DIM))
            vs_l.append(nv_s.reshape(db, WINDOW, N_KV_HEADS, HEAD_DIM))
        g_ffn = norm_ffn[i][None, :]
        wq = peer_w_q[i].astype(BF16)
        sk = peer_subkeys[i].astype(BF16)
        u_tab = peer_u[i].astype(BF16)
        v_tab = peer_v[i].astype(BF16)
        xp = _peer(xp.reshape(b * s, D_MODEL), g_ffn, wq, sk, u_tab, v_tab).reshape(b, s, D_MODEL)
        xs = _peer(xs, g_ffn, wq, sk, u_tab, v_tab)
    return (xp, xs.reshape(db, t, D_MODEL), jnp.stack(pool_p), jnp.stack(pool_s), jnp.stack(kp_l),
            jnp.stack(vp_l), jnp.stack(ks_l), jnp.stack(vs_l))
```

```python
import functools
import math

import jax
import jax.numpy as jnp
import numpy as np
from jax import lax
from jax.experimental import pallas as pl
from jax.experimental.pallas import tpu as pltpu

F32 = jnp.float32
BF16 = jnp.bfloat16

D_MODEL = 1024
LANES = 128
SUBLANES = 8
BF16_ROWS = 16
NORM_EPS = 1e-6
NEG_INF = float("-inf")

POOL_WINDOWS = (2, 4, 8, 16)
POOL_GROUP_DIM = D_MODEL // len(POOL_WINDOWS)
POOL_HIST = 16

HEAD_DIM = 64
N_HEADS = 16
N_KV_HEADS = 4
GQA_GROUP = N_HEADS // N_KV_HEADS
WINDOW = 128
ROPE_THETA = 10000.0
ATTN_SCALE = 1.0 / math.sqrt(HEAD_DIM)
Q_WIDTH = N_HEADS * HEAD_DIM
KV_WIDTH = N_KV_HEADS * HEAD_DIM
MASK_NEG = -1e30

PEER_HEADS = 8
PEER_NKEYS = 128
PEER_TOPK = 16
PEER_DKEY = 256

VMEM_LIMIT = 56 * 1024 * 1024


def _cparams(n_axes, **kw):
    return pltpu.CompilerParams(dimension_semantics=("arbitrary",) * n_axes,
                                vmem_limit_bytes=VMEM_LIMIT, **kw)


def _rmsnorm(x, g):
    return x * lax.rsqrt(jnp.mean(x * x, axis=-1, keepdims=True) + NORM_EPS) * g


def _dot(a, b):
    return jnp.dot(a, b, preferred_element_type=F32)


def _dot_nt(a, b):
    return lax.dot_general(a, b, (((1,), (1,)), ((), ())), preferred_element_type=F32)


def _dot_tn(a, b):
    return lax.dot_general(a, b, (((0,), (0,)), ((), ())), preferred_element_type=F32)


def _pool_prompt_kernel(x_ref, g_ref, win_ref, wg_ref, sc_ref, y_ref, st_ref, ext_s, *, ts):
    sb = pl.program_id(1)

    @pl.when(sb == 0)
    def _():
        ext_s[0:POOL_HIST, :] = jnp.zeros((POOL_HIST, D_MODEL), F32)

    x = x_ref[0]
    h = _rmsnorm(x, g_ref[...])
    u = _dot(h.astype(BF16), win_ref[...])
    ext_s[POOL_HIST:POOL_HIST + ts, :] = u
    pos = sb * ts + lax.broadcasted_iota(jnp.int32, (ts, 1), 0)
    outs = []
    for g, w in enumerate(POOL_WINDOWS):
        c0 = g * POOL_GROUP_DIM
        ug = u[:, c0:c0 + POOL_GROUP_DIM]
        win_sum = ug
        for j in range(1, w):
            win_sum = win_sum + ext_s[POOL_HIST - j:POOL_HIST - j + ts, c0:c0 + POOL_GROUP_DIM]
        count = jnp.minimum(pos + 1, w).astype(F32)
        pooled = win_sum / count - ug
        outs.append(_dot(pooled.astype(BF16), wg_ref[g]))
    y = jnp.concatenate(outs, axis=1) * sc_ref[...]
    y_ref[0] = x + y
    st_ref[0] = u[ts - POOL_HIST:ts, :]
    ext_s[0:POOL_HIST, :] = u[ts - POOL_HIST:ts, :]


def _pool_prompt(x, g, win, wg, sc, ts=512):
    b, s, _ = x.shape
    return pl.pallas_call(
        functools.partial(_pool_prompt_kernel, ts=ts),
        grid=(b, s // ts),
        in_specs=[
            pl.BlockSpec((1, ts, D_MODEL), lambda i, j: (i, j, 0)),
            pl.BlockSpec((1, D_MODEL), lambda i, j: (0, 0)),
            pl.BlockSpec((D_MODEL, D_MODEL), lambda i, j: (0, 0)),
            pl.BlockSpec((len(POOL_WINDOWS), POOL_GROUP_DIM, POOL_GROUP_DIM), lambda i, j: (0, 0, 0)),
            pl.BlockSpec((1, D_MODEL), lambda i, j: (0, 0)),
        ],
        out_specs=[
            pl.BlockSpec((1, ts, D_MODEL), lambda i, j: (i, j, 0)),
            pl.BlockSpec((1, POOL_HIST, D_MODEL), lambda i, j: (i, 0, 0)),
        ],
        out_shape=[
            jax.ShapeDtypeStruct((b, s, D_MODEL), F32),
            jax.ShapeDtypeStruct((b, POOL_HIST, D_MODEL), F32),
        ],
        scratch_shapes=[pltpu.VMEM((POOL_HIST + ts, D_MODEL), F32)],
        compiler_params=_cparams(2),
        name="pool_prompt",
    )(x, g, win, wg, sc)


def _pool_sample_kernel(x_ref, pr_ref, g_ref, win_ref, wg_ref, sc_ref, y_ref, st_ref, ext_s, *, bb, t):
    x = x_ref[...]
    h = _rmsnorm(x, g_ref[...])
    u = _dot(h.astype(BF16), win_ref[...])
    ext_s[:, 0:POOL_HIST, :] = pr_ref[...]
    ext_s[:, POOL_HIST:POOL_HIST + t, :] = u.reshape(bb, t, D_MODEL)
    outs = []
    for g, w in enumerate(POOL_WINDOWS):
        c0 = g * POOL_GROUP_DIM
        win_sum = ext_s[:, POOL_HIST:POOL_HIST + t, c0:c0 + POOL_GROUP_DIM]
        for j in range(1, w):
            win_sum = win_sum + ext_s[:, POOL_HIST - j:POOL_HIST - j + t, c0:c0 + POOL_GROUP_DIM]
        pooled = win_sum.reshape(bb * t, POOL_GROUP_DIM) / float(w) - u[:, c0:c0 + POOL_GROUP_DIM]
        outs.append(_dot(pooled.astype(BF16), wg_ref[g]))
    y = jnp.concatenate(outs, axis=1) * sc_ref[...]
    y_ref[...] = x + y
    st_ref[...] = ext_s[:, t:t + POOL_HIST, :]


def _pool_sample(x2, prior16, g, win, wg, sc, t, bb=32):
    n = x2.shape[0]
    nb = n // t
    return pl.pallas_call(
        functools.partial(_pool_sample_kernel, bb=bb, t=t),
        grid=(nb // bb,),
        in_specs=[
            pl.BlockSpec((bb * t, D_MODEL), lambda i: (i, 0)),
            pl.BlockSpec((bb, POOL_HIST, D_MODEL), lambda i: (i, 0, 0)),
            pl.BlockSpec((1, D_MODEL), lambda i: (0, 0)),
            pl.BlockSpec((D_MODEL, D_MODEL), lambda i: (0, 0)),
            pl.BlockSpec((len(POOL_WINDOWS), POOL_GROUP_DIM, POOL_GROUP_DIM), lambda i: (0, 0, 0)),
            pl.BlockSpec((1, D_MODEL), lambda i: (0, 0)),
        ],
        out_specs=[
            pl.BlockSpec((bb * t, D_MODEL), lambda i: (i, 0)),
            pl.BlockSpec((bb, POOL_HIST, D_MODEL), lambda i: (i, 0, 0)),
        ],
        out_shape=[
            jax.ShapeDtypeStruct((n, D_MODEL), F32),
            jax.ShapeDtypeStruct((nb, POOL_HIST, D_MODEL), F32),
        ],
        scratch_shapes=[pltpu.VMEM((bb, POOL_HIST + t, D_MODEL), F32)],
        compiler_params=_cparams(1),
        name="pool_sample",
    )(x2, prior16, g, win, wg, sc)


def _qkv_kernel(x_ref, g_ref, w_ref, seg_ref, gain_ref, cos_ref, sin_ref, q_ref, k_ref, v_ref):
    x = x_ref[...]
    h = _rmsnorm(x, g_ref[...])
    qkv = _dot(h.astype(BF16), w_ref[...])
    nqk = Q_WIDTH + KV_WIDTH
    qk = qkv[:, :nqk]
    sq = qk * qk
    sq_hi = sq.astype(BF16)
    sq_lo = (sq - sq_hi.astype(F32)).astype(BF16)
    seg = seg_ref[...]
    cw = seg.shape[0]
    parts = []
    for c in range(nqk // cw):
        sl = slice(c * cw, (c + 1) * cw)
        parts.append(_dot(sq_hi[:, sl], seg) + _dot(sq_lo[:, sl], seg))
    ssq = jnp.concatenate(parts, axis=1)
    qk = qk * lax.rsqrt(ssq * (1.0 / HEAD_DIM) + NORM_EPS) * gain_ref[...]
    half = HEAD_DIM // 2
    lane = lax.broadcasted_iota(jnp.int32, qk.shape, 1)
    first = (lane % HEAD_DIM) < half
    up = pltpu.roll(qk, nqk - half, 1)
    down = pltpu.roll(qk, half, 1)
    rot = jnp.where(first, -up, down)
    reps = nqk // cos_ref.shape[1]
    cos = jnp.concatenate([cos_ref[...]] * reps, axis=1)
    sin = jnp.concatenate([sin_ref[...]] * reps, axis=1)
    qk = qk * cos + rot * sin
    q_ref[...] = qk[:, :Q_WIDTH]
    k_ref[...] = qk[:, Q_WIDTH:]
    v_ref[...] = qkv[:, nqk:]


def _qkv(x2, g, w, seg, gain, cos, sin, tq, pos_blocks):
    n = x2.shape[0]
    tw = cos.shape[1]
    return pl.pallas_call(
        _qkv_kernel,
        grid=(n // tq,),
        in_specs=[
            pl.BlockSpec((tq, D_MODEL), lambda i: (i, 0)),
            pl.BlockSpec((1, D_MODEL), lambda i: (0, 0)),
            pl.BlockSpec(w.shape, lambda i: (0, 0)),
            pl.BlockSpec(seg.shape, lambda i: (0, 0)),
            pl.BlockSpec((1, Q_WIDTH + KV_WIDTH), lambda i: (0, 0)),
            pl.BlockSpec((tq, tw), lambda i: (i % pos_blocks, 0)),
            pl.BlockSpec((tq, tw), lambda i: (i % pos_blocks, 0)),
        ],
        out_specs=[
            pl.BlockSpec((tq, Q_WIDTH), lambda i: (i, 0)),
            pl.BlockSpec((tq, KV_WIDTH), lambda i: (i, 0)),
            pl.BlockSpec((tq, KV_WIDTH), lambda i: (i, 0)),
        ],
        out_shape=[
            jax.ShapeDtypeStruct((n, Q_WIDTH), F32),
            jax.ShapeDtypeStruct((n, KV_WIDTH), F32),
            jax.ShapeDtypeStruct((n, KV_WIDTH), F32),
        ],
        compiler_params=_cparams(1),
        name="attn_qkv",
    )(x2, g, w, seg, gain, cos, sin)


def _attn_prompt_kernel(x_ref, q_ref, kp_ref, kc_ref, vp_ref, vc_ref, sink_ref, wo_ref, y_ref):
    nb = pl.program_id(1)
    q = q_ref[0].astype(BF16)
    k2 = jnp.concatenate([kp_ref[0], kc_ref[0]], axis=0).astype(BF16)
    v2 = jnp.concatenate([vp_ref[0], vc_ref[0]], axis=0).astype(BF16)
    rows = GQA_GROUP * WINDOW
    qi = lax.broadcasted_iota(jnp.int32, (rows, 2 * WINDOW), 0) % WINDOW
    kj = lax.broadcasted_iota(jnp.int32, (rows, 2 * WINDOW), 1)
    mask = (kj > qi) & (kj <= qi + WINDOW) & (nb * WINDOW - WINDOW + kj >= 0)
    outs = []
    for kh in range(N_KV_HEADS):
        kk = k2[:, kh * HEAD_DIM:(kh + 1) * HEAD_DIM]
        vv = v2[:, kh * HEAD_DIM:(kh + 1) * HEAD_DIM]
        q4 = jnp.concatenate(
            [q[:, (kh * GQA_GROUP + g) * HEAD_DIM:(kh * GQA_GROUP + g + 1) * HEAD_DIM] for g in range(GQA_GROUP)],
            axis=0)
        sk = jnp.concatenate(
            [jnp.full((WINDOW, 1), 1.0, F32) * sink_ref[kh * GQA_GROUP + g] for g in range(GQA_GROUP)], axis=0)
        s = _dot_nt(q4, kk) * ATTN_SCALE
        s = jnp.where(mask, s, MASK_NEG)
        m = jnp.maximum(jnp.max(s, axis=-1, keepdims=True), sk)
        p = jnp.exp(s - m)
        p = p / (jnp.sum(p, axis=-1, keepdims=True) + jnp.exp(sk - m))
        o4 = _dot(p.astype(BF16), vv)
        for g in range(GQA_GROUP):
            outs.append(o4[g * WINDOW:(g + 1) * WINDOW, :])
    o = jnp.concatenate(outs, axis=1)
    y_ref[0] = x_ref[0] + _dot(o.astype(BF16), wo_ref[...])


def _attn_prompt(x, q, k, v, sinks, wo):
    b, s, _ = x.shape
    nblk = s // WINDOW
    cur = lambda i, j: (i, j, 0)
    prev = lambda i, j: (i, jnp.maximum(j - 1, 0), 0)
    return pl.pallas_call(
        _attn_prompt_kernel,
        grid=(b, nblk),
        in_specs=[
            pl.BlockSpec((1, WINDOW, D_MODEL), cur),
            pl.BlockSpec((1, WINDOW, Q_WIDTH), cur),
            pl.BlockSpec((1, WINDOW, KV_WIDTH), prev),
            pl.BlockSpec((1, WINDOW, KV_WIDTH), cur),
            pl.BlockSpec((1, WINDOW, KV_WIDTH), prev),
            pl.BlockSpec((1, WINDOW, KV_WIDTH), cur),
            pl.BlockSpec(memory_space=pltpu.SMEM),
            pl.BlockSpec((Q_WIDTH, D_MODEL), lambda i, j: (0, 0)),
        ],
        out_specs=pl.BlockSpec((1, WINDOW, D_MODEL), cur),
        out_shape=jax.ShapeDtypeStruct((b, s, D_MODEL), F32),
        compiler_params=_cparams(2),
        name="attn_prompt",
    )(x, q, k, k, v, v, sinks, wo)


def _attn_sample_kernel(x_ref, q_ref, kn_ref, vn_ref, ck_ref, cv_ref, sink_ref, wo_ref,
                        y_ref, nk_ref, nv_ref, *, bb, t):
    q = q_ref[...].reshape(bb, t, Q_WIDTH)
    kn = kn_ref[...].reshape(bb, t, KV_WIDTH)
    vn = vn_ref[...].reshape(bb, t, KV_WIDTH)
    ck = ck_ref[...]
    cv = cv_ref[...]
    nk_ref[:, 0:WINDOW - t, :] = ck[:, t:, :]
    nk_ref[:, WINDOW - t:, :] = kn
    nv_ref[:, 0:WINDOW - t, :] = cv[:, t:, :]
    nv_ref[:, WINDOW - t:, :] = vn
    rows = GQA_GROUP * t
    qi_c = lax.broadcasted_iota(jnp.int32, (bb, rows, WINDOW), 1) % t
    kj_c = lax.broadcasted_iota(jnp.int32, (bb, rows, WINDOW), 2)
    mask_c = kj_c > qi_c
    qi_n = lax.broadcasted_iota(jnp.int32, (bb, rows, t), 1) % t
    kj_n = lax.broadcasted_iota(jnp.int32, (bb, rows, t), 2)
    mask_n = kj_n <= qi_n
    ckb, cvb, knb, vnb = ck.astype(BF16), cv.astype(BF16), kn.astype(BF16), vn.astype(BF16)
    outs = []
    for kh in range(N_KV_HEADS):
        hs = slice(kh * HEAD_DIM, (kh + 1) * HEAD_DIM)
        q4 = jnp.concatenate(
            [q[:, :, (kh * GQA_GROUP + g) * HEAD_DIM:(kh * GQA_GROUP + g + 1) * HEAD_DIM] for g in range(GQA_GROUP)],
            axis=1).astype(BF16)
        sk = jnp.concatenate(
            [jnp.full((1, t, 1), 1.0, F32) * sink_ref[kh * GQA_GROUP + g] for g in range(GQA_GROUP)], axis=1)
        s_c = jnp.einsum("bqd,bkd->bqk", q4, ckb[:, :, hs], preferred_element_type=F32) * ATTN_SCALE
        s_n = jnp.einsum("bqd,bkd->bqk", q4, knb[:, :, hs], preferred_element_type=F32) * ATTN_SCALE
        s_c = jnp.where(mask_c, s_c, MASK_NEG)
        s_n = jnp.where(mask_n, s_n, MASK_NEG)
        m = jnp.maximum(jnp.maximum(jnp.max(s_c, axis=-1, keepdims=True),
                                    jnp.max(s_n, axis=-1, keepdims=True)), sk)
        p_c = jnp.exp(s_c - m)
        p_n = jnp.exp(s_n - m)
        den = jnp.sum(p_c, axis=-1, keepdims=True) + jnp.sum(p_n, axis=-1, keepdims=True) + jnp.exp(sk - m)
        inv = 1.0 / den
        o4 = (jnp.einsum("bqk,bkd->bqd", (p_c * inv).astype(BF16), cvb[:, :, hs], preferred_element_type=F32)
              + jnp.einsum("bqk,bkd->bqd", (p_n * inv).astype(BF16), vnb[:, :, hs], preferred_element_type=F32))
        for g in range(GQA_GROUP):
            outs.append(o4[:, g * t:(g + 1) * t, :])
    o = jnp.concatenate(outs, axis=2).reshape(bb * t, Q_WIDTH)
    y_ref[...] = x_ref[...] + _dot(o.astype(BF16), wo_ref[...])


def _attn_sample(x2, q, kn, vn, ck, cv, sinks, wo, t, bb=16):
    n = x2.shape[0]
    nb = n // t
    row = lambda i: (i, 0)
    cache = lambda i: (i, 0, 0)
    return pl.pallas_call(
        functools.partial(_attn_sample_kernel, bb=bb, t=t),
        grid=(nb // bb,),
        in_specs=[
            pl.BlockSpec((bb * t, D_MODEL), row),
            pl.BlockSpec((bb * t, Q_WIDTH), row),
            pl.BlockSpec((bb * t, KV_WIDTH), row),
            pl.BlockSpec((bb * t, KV_WIDTH), row),
            pl.BlockSpec((bb, WINDOW, KV_WIDTH), cache),
            pl.BlockSpec((bb, WINDOW, KV_WIDTH), cache),
            pl.BlockSpec(memory_space=pltpu.SMEM),
            pl.BlockSpec((Q_WIDTH, D_MODEL), lambda i: (0, 0)),
        ],
        out_specs=[
            pl.BlockSpec((bb * t, D_MODEL), row),
            pl.BlockSpec((bb, WINDOW, KV_WIDTH), cache),
            pl.BlockSpec((bb, WINDOW, KV_WIDTH), cache),
        ],
        out_shape=[
            jax.ShapeDtypeStruct((n, D_MODEL), F32),
            jax.ShapeDtypeStruct((nb, WINDOW, KV_WIDTH), F32),
            jax.ShapeDtypeStruct((nb, WINDOW, KV_WIDTH), F32),
        ],
        compiler_params=_cparams(1),
        name="attn_sample",
    )(x2, q, kn, vn, ck, cv, sinks, wo)


_CAND_LIMITS = tuple(PEER_TOPK // (r2 + 1) for r2 in range(1, SUBLANES))


def _sorting_network(n):
    pairs = []
    p = 1
    while p < n:
        k = p
        while k >= 1:
            for j in range(k % p, n - k, 2 * k):
                for i in range(min(k, n - j - k)):
                    if (i + j) // (2 * p) == (i + j + k) // (2 * p):
                        pairs.append((i + j, i + j + k))
            k //= 2
        p *= 2
    return tuple(pairs)


_SORT_PAIRS = _sorting_network(PEER_TOPK)


def _compare_exchange(x, i, j):
    if x[j] is not None:
        x[i], x[j] = jnp.maximum(x[i], x[j]), jnp.minimum(x[i], x[j])


def _top_sorted(tiles):
    n = len(tiles)
    x = list(tiles) + [None] * (PEER_TOPK - n)
    for i, j in _SORT_PAIRS:
        if j < n:
            _compare_exchange(x, i, j)
    for shift in (4, 2, 1):
        y = [None if v is None else pltpu.roll(v, shift, 0) for v in x]
        m = []
        for k in range(PEER_TOPK):
            a, b = x[k], y[PEER_TOPK - 1 - k]
            m.append(b if a is None else a if b is None else jnp.maximum(a, b))
        d = PEER_TOPK // 2
        while d:
            for k in range(PEER_TOPK):
                if not k & d:
                    _compare_exchange(m, k, k + d)
            d //= 2
        x = m
    return x


def _peer_select(s1, s2, cp_s):
    nt8 = PEER_NKEYS // SUBLANES
    s1 = s1.reshape(nt8, SUBLANES, LANES)
    s2 = s2.reshape(nt8, SUBLANES, LANES)
    v1 = _top_sorted([s1[i] for i in range(nt8)])
    v2 = _top_sorted([s2[i] for i in range(nt8)])
    row8 = lax.broadcasted_iota(jnp.int32, (SUBLANES, LANES), 0)

    def column(v, start):
        col = v[start]
        for s in range(1, SUBLANES):
            col = jnp.where(row8 == s, v[start + s], col)
        return col

    v1_lo, v1_hi, v2_hi = column(v1, 0), column(v1, SUBLANES), column(v2, SUBLANES)
    cand = [v1_lo + v2[0], v1_hi + v2[0]]
    for r2, lim in enumerate(_CAND_LIMITS, start=1):
        g = v1_lo + v2[r2]
        cand.append(jnp.where(row8 < lim, g, NEG_INF) if lim < SUBLANES else g)
    cand.append(v2_hi + v1[0])
    tau = _top_sorted(cand)[PEER_TOPK - 1]
    best = v1[0] + v2[0]
    sel = [c >= tau for c in cand]
    zt = jnp.zeros((SUBLANES, LANES), F32)
    for c, m in zip(cand, sel):
        zt = zt + jnp.where(m, jnp.exp(c - best), 0.0)
    z = jnp.sum(zt, axis=0, keepdims=True)
    self32 = [m.astype(F32) for m in sel]
    lo = self32[0]
    for t in self32[2:-1]:
        lo = lo + t
    tail = jnp.sum(self32[-1], axis=0, keepdims=True)
    cp_s[0:SUBLANES, :] = lo + jnp.where(row8 == 0, tail, 0.0)
    cp_s[SUBLANES:, :] = self32[1]
    count1 = jnp.zeros_like(s1)
    rank2 = jnp.zeros_like(s2)
    for r in range(PEER_TOPK):
        count1 = jnp.maximum(count1, jnp.where(s1 >= v1[r], cp_s[r:r + 1, :], 0.0))
        rank2 = rank2 + jnp.where(v2[r] > s2, 1.0, 0.0)
    e2 = jnp.exp(s2 - v2[0])
    e1z = jnp.exp(s1 - v1[0]) * (1.0 / z)
    return rank2, e2, count1, e1z


def _peer_kernel(x_ref, g_ref, wq_ref, sk_ref, u_ref, v_ref, o_ref,
                 hn_s, s_s, cp_s, r2_s, e2_s, c1_s, e1_s, a0_s, a1_s, w_s, y_s,
                 *, tm, te, tc, kb, nt):
    j = pl.program_id(1)
    nj = pl.num_programs(1)
    nlc = tm // LANES
    gi = te // PEER_NKEYS
    packed = (PEER_NKEYS // BF16_ROWS, BF16_ROWS, LANES)

    @pl.when(j == 0)
    def _():
        hn = _rmsnorm(x_ref[...], g_ref[...]).astype(BF16)
        hn_s[...] = hn
        for h in range(PEER_HEADS):
            qh = _dot(hn, wq_ref[:, h * PEER_DKEY:(h + 1) * PEER_DKEY]).astype(BF16)
            for p in range(2):
                half = PEER_DKEY // 2
                st = _dot_nt(sk_ref[h, p], qh[:, p * half:(p + 1) * half])
                for ck in range(nlc):
                    s_s[2 * h + p, ck] = st[:, ck * LANES:(ck + 1) * LANES]

        def select(idx, carry):
            h = idx // nlc
            ck = idx % nlc
            rank2, e2, count1, e1z = _peer_select(s_s[2 * h, ck], s_s[2 * h + 1, ck], cp_s)
            flat = (PEER_NKEYS, LANES)
            r2_s[h, ck] = rank2.reshape(flat).astype(BF16).reshape(packed)
            e2_s[h, ck] = e2.reshape(flat).astype(BF16).reshape(packed)
            c1_s[h, ck] = count1.reshape(flat)
            e1_s[h, ck] = e1z.reshape(flat)
            return carry

        lax.fori_loop(0, PEER_HEADS * nlc, select, 0)
        y_s[...] = jnp.zeros_like(y_s)
        a1_s[...] = jnp.zeros_like(a1_s)

    i1base = jnp.maximum(j - 1, 0) * gi
    per_kb = kb // PEER_NKEYS

    def step(a_w, a_r):
        for c in range(tm // tc):
            tok = slice(c * tc, (c + 1) * tc)
            a_w[c] = _dot_nt(u_ref[...], hn_s[tok, :])
        for b in range(te // kb):
            grp, l0 = divmod(b * per_kb, SUBLANES)
            i1s = pl.ds(pl.multiple_of(i1base + grp * SUBLANES, SUBLANES), SUBLANES)
            for ck in range(nlc):
                c, lc = divmod(ck, tc // LANES)
                ls = slice(lc * LANES, (lc + 1) * LANES)
                cnt = [c1_s[h, ck, i1s, :] for h in range(PEER_HEADS)]
                e1 = [e1_s[h, ck, i1s, :] for h in range(PEER_HEADS)]
                for l in range(l0, l0 + per_kb):
                    acc = jnp.zeros(packed, BF16)
                    for h in range(PEER_HEADS):
                        cb = jnp.broadcast_to(cnt[h][l:l + 1], (BF16_ROWS, LANES)).astype(BF16)
                        eb = jnp.broadcast_to(e1[h][l:l + 1], (BF16_ROWS, LANES)).astype(BF16)
                        acc = acc + jnp.where(r2_s[h, ck] < cb, e2_s[h, ck], 0.0) * eb
                    rs = slice((grp * SUBLANES + l) * PEER_NKEYS, (grp * SUBLANES + l + 1) * PEER_NKEYS)
                    a = a_r[c, rs, ls]
                    gelu = 0.5 * a * (1.0 + lax.erf(a * math.sqrt(0.5)))
                    w = (acc * gelu.astype(BF16).reshape(packed)).reshape(PEER_NKEYS, LANES)
                    w_s[ck * LANES:(ck + 1) * LANES, rs] = w.T
            es = slice(b * kb, (b + 1) * kb)
            y_s[...] += _dot(w_s[:, es], v_ref[es, :])

    @pl.when(j % 2 == 0)
    def _():
        step(a0_s, a1_s)

    @pl.when(j % 2 == 1)
    def _():
        step(a1_s, a0_s)

    @pl.when(j == nj - 1)
    def _():
        o_ref[...] = x_ref[...] + y_s[...]


def _peer(x2, g, wq, sk, u_tab, v_tab, tm=512, te=1024, tc=256, kb=256):
    n = x2.shape[0]
    nt = u_tab.shape[0] // te
    nlc = tm // LANES
    packed = (PEER_NKEYS // BF16_ROWS, BF16_ROWS, LANES)
    return pl.pallas_call(
        functools.partial(_peer_kernel, tm=tm, te=te, tc=tc, kb=kb, nt=nt),
        grid=(n // tm, nt + 1),
        in_specs=[
            pl.BlockSpec((tm, D_MODEL), lambda i, j: (i, 0)),
            pl.BlockSpec((1, D_MODEL), lambda i, j: (0, 0)),
            pl.BlockSpec(wq.shape, lambda i, j: (0, 0)),
            pl.BlockSpec(sk.shape, lambda i, j: (0, 0, 0, 0)),
            pl.BlockSpec((te, D_MODEL), lambda i, j: (jnp.minimum(j, nt - 1), 0)),
            pl.BlockSpec((te, D_MODEL), lambda i, j: (jnp.maximum(j - 1, 0), 0)),
        ],
        out_specs=pl.BlockSpec((tm, D_MODEL), lambda i, j: (i, 0)),
        out_shape=jax.ShapeDtypeStruct((n, D_MODEL), F32),
        scratch_shapes=[
            pltpu.VMEM((tm, D_MODEL), BF16),
            pltpu.VMEM((2 * PEER_HEADS, nlc, PEER_NKEYS, LANES), F32),
            pltpu.VMEM((PEER_TOPK, LANES), F32),
            pltpu.VMEM((PEER_HEADS, nlc) + packed, BF16),
            pltpu.VMEM((PEER_HEADS, nlc) + packed, BF16),
            pltpu.VMEM((PEER_HEADS, nlc, PEER_NKEYS, LANES), F32),
            pltpu.VMEM((PEER_HEADS, nlc, PEER_NKEYS, LANES), F32),
            pltpu.VMEM((tm // tc, te, tc), F32),
            pltpu.VMEM((tm // tc, te, tc), F32),
            pltpu.VMEM((tm, te), BF16),
            pltpu.VMEM((tm, D_MODEL), F32),
        ],
        compiler_params=_cparams(2),
        name="peer",
    )(x2, g, wq, sk, u_tab, v_tab)


def _rope_tables(pos):
    half = HEAD_DIM // 2
    inv_freq = jnp.power(jnp.float32(ROPE_THETA), -jnp.arange(half, dtype=F32) * (2.0 / HEAD_DIM))
    ang = pos.astype(F32)[:, None] * inv_freq[None, :]
    cos = jnp.concatenate([jnp.cos(ang)] * 4, axis=1)
    sin = jnp.concatenate([jnp.sin(ang)] * 4, axis=1)
    return cos, sin


def kernel(x_prompt, x_sample, state_pool, cache_k, cache_v, norm_mix, norm_ffn, pool_w_in, pool_w_group,
           pool_scale, attn_w_qkv, attn_q_norm, attn_k_norm, attn_sinks, attn_w_o, peer_w_q, peer_subkeys,
           peer_u, peer_v):
    b, s, _ = x_prompt.shape
    db, t, _ = x_sample.shape
    past_len = 16384
    depth = norm_mix.shape[0]
    xp = x_prompt
    xs = x_sample.reshape(db * t, D_MODEL)
    seg = jnp.asarray(np.kron(np.eye(256 // HEAD_DIM), np.ones((HEAD_DIM, HEAD_DIM))), BF16)
    cos_p, sin_p = _rope_tables(jnp.arange(s, dtype=jnp.int32))
    cos_s, sin_s = _rope_tables(past_len + jnp.arange(t, dtype=jnp.int32))
    pool_p, pool_s, kp_l, vp_l, ks_l, vs_l = [], [], [], [], [], []
    for i in range(depth):
        jl = i // 2
        g_mix = norm_mix[i][None, :]
        if i % 2 == 0:
            win = pool_w_in[jl].astype(BF16)
            wg = pool_w_group[jl].astype(BF16)
            sc = pool_scale[jl][None, :]
            xp, st_p = _pool_prompt(xp, g_mix, win, wg, sc)
            prior16 = jnp.pad(state_pool[jl], ((0, 0), (1, 0), (0, 0)))
            xs, st_s = _pool_sample(xs, prior16, g_mix, win, wg, sc, t)
            pool_p.append(st_p[:, 1:])
            pool_s.append(st_s[:, 1:])
        else:
            wqkv = attn_w_qkv[jl].astype(BF16)
            wo = attn_w_o[jl].astype(BF16)
            gain = jnp.concatenate([jnp.tile(attn_q_norm[jl], N_HEADS), jnp.tile(attn_k_norm[jl], N_KV_HEADS)])[None, :]
            sinks = attn_sinks[jl]
            tq = 512
            qp, kp, vp = _qkv(xp.reshape(b * s, D_MODEL), g_mix, wqkv, seg, gain, cos_p, sin_p, tq, s // tq)
            kp3 = kp.reshape(b, s, KV_WIDTH)
            vp3 = vp.reshape(b, s, KV_WIDTH)
            xp = _attn_prompt(xp, qp.reshape(b, s, Q_WIDTH), kp3, vp3, sinks, wo)
            cos_st = jnp.tile(cos_s, (tq // t, 1))
            sin_st = jnp.tile(sin_s, (tq // t, 1))
            qs, kq, vq = _qkv(xs, g_mix, wqkv, seg, gain, cos_st, sin_st, tq, 1)
            ck = cache_k[jl].reshape(db, WINDOW, KV_WIDTH)
            cv = cache_v[jl].reshape(db, WINDOW, KV_WIDTH)
            xs, nk_s, nv_s = _attn_sample(xs, qs, kq, vq, ck, cv, sinks, wo, t)
            kp_l.append(kp3[:, -WINDOW:].reshape(b, WINDOW, N_KV_HEADS, HEAD_DIM))
            vp_l.append(vp3[:, -WINDOW:].reshape(b, WINDOW, N_KV_HEADS, HEAD_DIM))
            ks_l.append(nk_s.reshape(db, WINDOW, N_KV_HEADS, HEAD_DIM))
            vs_l.append(nv_s.reshape(db, WINDOW, N_KV_HEADS, HEAD_DIM))
        g_ffn = norm_ffn[i][None, :]
        wq = peer_w_q[i].astype(BF16)
        sk = peer_subkeys[i].astype(BF16)
        u_tab = peer_u[i].astype(BF16)
        v_tab = peer_v[i].astype(BF16)
        xp = _peer(xp.reshape(b * s, D_MODEL), g_ffn, wq, sk, u_tab, v_tab).reshape(b, s, D_MODEL)
        xs = _peer(xs, g_ffn, wq, sk, u_tab, v_tab)
    return (xp, xs.reshape(db, t, D_MODEL), jnp.stack(pool_p), jnp.stack(pool_s), jnp.stack(kp_l),
            jnp.stack(vp_l), jnp.stack(ks_l), jnp.stack(vs_l))
```

```python
import functools
import math

import jax
import jax.numpy as jnp
import numpy as np
from jax import lax
from jax.experimental import pallas as pl
from jax.experimental.pallas import tpu as pltpu

F32 = jnp.float32
BF16 = jnp.bfloat16

D_MODEL = 1024
LANES = 128
SUBLANES = 8
GATE_DTYPE = jnp.float32
GATE_ROWS = 8
NORM_EPS = 1e-6
NEG_INF = float("-inf")

POOL_WINDOWS = (2, 4, 8, 16)
POOL_GROUP_DIM = D_MODEL // len(POOL_WINDOWS)
POOL_HIST = 16

HEAD_DIM = 64
N_HEADS = 16
N_KV_HEADS = 4
GQA_GROUP = N_HEADS // N_KV_HEADS
WINDOW = 128
ROPE_THETA = 10000.0
ATTN_SCALE = 1.0 / math.sqrt(HEAD_DIM)
Q_WIDTH = N_HEADS * HEAD_DIM
KV_WIDTH = N_KV_HEADS * HEAD_DIM
MASK_NEG = -1e30

PEER_HEADS = 8
PEER_NKEYS = 128
PEER_TOPK = 16
PEER_DKEY = 256

VMEM_LIMIT = 56 * 1024 * 1024


def _cparams(n_axes, **kw):
    return pltpu.CompilerParams(dimension_semantics=("arbitrary",) * n_axes,
                                vmem_limit_bytes=VMEM_LIMIT, **kw)


def _rmsnorm(x, g):
    return x * lax.rsqrt(jnp.mean(x * x, axis=-1, keepdims=True) + NORM_EPS) * g


def _dot(a, b):
    return jnp.dot(a, b, preferred_element_type=F32)


def _dot_nt(a, b):
    return lax.dot_general(a, b, (((1,), (1,)), ((), ())), preferred_element_type=F32)


def _dot_tn(a, b):
    return lax.dot_general(a, b, (((0,), (0,)), ((), ())), preferred_element_type=F32)


def _pool_prompt_kernel(x_ref, g_ref, win_ref, wg_ref, sc_ref, y_ref, st_ref, ext_s, *, ts):
    sb = pl.program_id(1)

    @pl.when(sb == 0)
    def _():
        ext_s[0:POOL_HIST, :] = jnp.zeros((POOL_HIST, D_MODEL), F32)

    x = x_ref[0]
    h = _rmsnorm(x, g_ref[...])
    u = _dot(h.astype(BF16), win_ref[...])
    ext_s[POOL_HIST:POOL_HIST + ts, :] = u
    pos = sb * ts + lax.broadcasted_iota(jnp.int32, (ts, 1), 0)
    outs = []
    for g, w in enumerate(POOL_WINDOWS):
        c0 = g * POOL_GROUP_DIM
        ug = u[:, c0:c0 + POOL_GROUP_DIM]
        win_sum = ug
        for j in range(1, w):
            win_sum = win_sum + ext_s[POOL_HIST - j:POOL_HIST - j + ts, c0:c0 + POOL_GROUP_DIM]
        count = jnp.minimum(pos + 1, w).astype(F32)
        pooled = win_sum / count - ug
        outs.append(_dot(pooled.astype(BF16), wg_ref[g]))
    y = jnp.concatenate(outs, axis=1) * sc_ref[...]
    y_ref[0] = x + y
    st_ref[0] = u[ts - POOL_HIST:ts, :]
    ext_s[0:POOL_HIST, :] = u[ts - POOL_HIST:ts, :]


def _pool_prompt(x, g, win, wg, sc, ts=512):
    b, s, _ = x.shape
    return pl.pallas_call(
        functools.partial(_pool_prompt_kernel, ts=ts),
        grid=(b, s // ts),
        in_specs=[
            pl.BlockSpec((1, ts, D_MODEL), lambda i, j: (i, j, 0)),
            pl.BlockSpec((1, D_MODEL), lambda i, j: (0, 0)),
            pl.BlockSpec((D_MODEL, D_MODEL), lambda i, j: (0, 0)),
            pl.BlockSpec((len(POOL_WINDOWS), POOL_GROUP_DIM, POOL_GROUP_DIM), lambda i, j: (0, 0, 0)),
            pl.BlockSpec((1, D_MODEL), lambda i, j: (0, 0)),
        ],
        out_specs=[
            pl.BlockSpec((1, ts, D_MODEL), lambda i, j: (i, j, 0)),
            pl.BlockSpec((1, POOL_HIST, D_MODEL), lambda i, j: (i, 0, 0)),
        ],
        out_shape=[
            jax.ShapeDtypeStruct((b, s, D_MODEL), F32),
            jax.ShapeDtypeStruct((b, POOL_HIST, D_MODEL), F32),
        ],
        scratch_shapes=[pltpu.VMEM((POOL_HIST + ts, D_MODEL), F32)],
        compiler_params=_cparams(2),
        name="pool_prompt",
    )(x, g, win, wg, sc)


def _pool_sample_kernel(x_ref, pr_ref, g_ref, win_ref, wg_ref, sc_ref, y_ref, st_ref, ext_s, *, bb, t):
    x = x_ref[...]
    h = _rmsnorm(x, g_ref[...])
    u = _dot(h.astype(BF16), win_ref[...])
    ext_s[:, 0:POOL_HIST, :] = pr_ref[...]
    ext_s[:, POOL_HIST:POOL_HIST + t, :] = u.reshape(bb, t, D_MODEL)
    outs = []
    for g, w in enumerate(POOL_WINDOWS):
        c0 = g * POOL_GROUP_DIM
        win_sum = ext_s[:, POOL_HIST:POOL_HIST + t, c0:c0 + POOL_GROUP_DIM]
        for j in range(1, w):
            win_sum = win_sum + ext_s[:, POOL_HIST - j:POOL_HIST - j + t, c0:c0 + POOL_GROUP_DIM]
        pooled = win_sum.reshape(bb * t, POOL_GROUP_DIM) / float(w) - u[:, c0:c0 + POOL_GROUP_DIM]
        outs.append(_dot(pooled.astype(BF16), wg_ref[g]))
    y = jnp.concatenate(outs, axis=1) * sc_ref[...]
    y_ref[...] = x + y
    st_ref[...] = ext_s[:, t:t + POOL_HIST, :]


def _pool_sample(x2, prior16, g, win, wg, sc, t, bb=32):
    n = x2.shape[0]
    nb = n // t
    return pl.pallas_call(
        functools.partial(_pool_sample_kernel, bb=bb, t=t),
        grid=(nb // bb,),
        in_specs=[
            pl.BlockSpec((bb * t, D_MODEL), lambda i: (i, 0)),
            pl.BlockSpec((bb, POOL_HIST, D_MODEL), lambda i: (i, 0, 0)),
            pl.BlockSpec((1, D_MODEL), lambda i: (0, 0)),
            pl.BlockSpec((D_MODEL, D_MODEL), lambda i: (0, 0)),
            pl.BlockSpec((len(POOL_WINDOWS), POOL_GROUP_DIM, POOL_GROUP_DIM), lambda i: (0, 0, 0)),
            pl.BlockSpec((1, D_MODEL), lambda i: (0, 0)),
        ],
        out_specs=[
            pl.BlockSpec((bb * t, D_MODEL), lambda i: (i, 0)),
            pl.BlockSpec((bb, POOL_HIST, D_MODEL), lambda i: (i, 0, 0)),
        ],
        out_shape=[
            jax.ShapeDtypeStruct((n, D_MODEL), F32),
            jax.ShapeDtypeStruct((nb, POOL_HIST, D_MODEL), F32),
        ],
        scratch_shapes=[pltpu.VMEM((bb, POOL_HIST + t, D_MODEL), F32)],
        compiler_params=_cparams(1),
        name="pool_sample",
    )(x2, prior16, g, win, wg, sc)


def _qkv_kernel(x_ref, g_ref, w_ref, seg_ref, gain_ref, cos_ref, sin_ref, q_ref, k_ref, v_ref):
    x = x_ref[...]
    h = _rmsnorm(x, g_ref[...])
    qkv = _dot(h.astype(BF16), w_ref[...])
    nqk = Q_WIDTH + KV_WIDTH
    qk = qkv[:, :nqk]
    sq = qk * qk
    sq_hi = sq.astype(BF16)
    sq_lo = (sq - sq_hi.astype(F32)).astype(BF16)
    seg = seg_ref[...]
    cw = seg.shape[0]
    parts = []
    for c in range(nqk // cw):
        sl = slice(c * cw, (c + 1) * cw)
        parts.append(_dot(sq_hi[:, sl], seg) + _dot(sq_lo[:, sl], seg))
    ssq = jnp.concatenate(parts, axis=1)
    qk = qk * lax.rsqrt(ssq * (1.0 / HEAD_DIM) + NORM_EPS) * gain_ref[...]
    half = HEAD_DIM // 2
    lane = lax.broadcasted_iota(jnp.int32, qk.shape, 1)
    first = (lane % HEAD_DIM) < half
    up = pltpu.roll(qk, nqk - half, 1)
    down = pltpu.roll(qk, half, 1)
    rot = jnp.where(first, -up, down)
    reps = nqk // cos_ref.shape[1]
    cos = jnp.concatenate([cos_ref[...]] * reps, axis=1)
    sin = jnp.concatenate([sin_ref[...]] * reps, axis=1)
    qk = qk * cos + rot * sin
    q_ref[...] = qk[:, :Q_WIDTH]
    k_ref[...] = qk[:, Q_WIDTH:]
    v_ref[...] = qkv[:, nqk:]


def _qkv(x2, g, w, seg, gain, cos, sin, tq, pos_blocks):
    n = x2.shape[0]
    tw = cos.shape[1]
    return pl.pallas_call(
        _qkv_kernel,
        grid=(n // tq,),
        in_specs=[
            pl.BlockSpec((tq, D_MODEL), lambda i: (i, 0)),
            pl.BlockSpec((1, D_MODEL), lambda i: (0, 0)),
            pl.BlockSpec(w.shape, lambda i: (0, 0)),
            pl.BlockSpec(seg.shape, lambda i: (0, 0)),
            pl.BlockSpec((1, Q_WIDTH + KV_WIDTH), lambda i: (0, 0)),
            pl.BlockSpec((tq, tw), lambda i: (i % pos_blocks, 0)),
            pl.BlockSpec((tq, tw), lambda i: (i % pos_blocks, 0)),
        ],
        out_specs=[
            pl.BlockSpec((tq, Q_WIDTH), lambda i: (i, 0)),
            pl.BlockSpec((tq, KV_WIDTH), lambda i: (i, 0)),
            pl.BlockSpec((tq, KV_WIDTH), lambda i: (i, 0)),
        ],
        out_shape=[
            jax.ShapeDtypeStruct((n, Q_WIDTH), F32),
            jax.ShapeDtypeStruct((n, KV_WIDTH), F32),
            jax.ShapeDtypeStruct((n, KV_WIDTH), F32),
        ],
        compiler_params=_cparams(1),
        name="attn_qkv",
    )(x2, g, w, seg, gain, cos, sin)


def _attn_prompt_kernel(x_ref, q_ref, kp_ref, kc_ref, vp_ref, vc_ref, sink_ref, wo_ref, y_ref, *, nsub):
    nb0 = pl.program_id(1) * nsub
    q_all = q_ref[0].astype(BF16)
    k_all = jnp.concatenate([kp_ref[0], kc_ref[0]], axis=0).astype(BF16)
    v_all = jnp.concatenate([vp_ref[0], vc_ref[0]], axis=0).astype(BF16)
    rows = GQA_GROUP * WINDOW
    qi = lax.broadcasted_iota(jnp.int32, (rows, 2 * WINDOW), 0) % WINDOW
    kj = lax.broadcasted_iota(jnp.int32, (rows, 2 * WINDOW), 1)
    band = (kj > qi) & (kj <= qi + WINDOW)
    o_blocks = []
    for sub in range(nsub):
        q = q_all[sub * WINDOW:(sub + 1) * WINDOW]
        k2 = k_all[sub * WINDOW:(sub + 2) * WINDOW]
        v2 = v_all[sub * WINDOW:(sub + 2) * WINDOW]
        mask = band & ((nb0 + sub) * WINDOW - WINDOW + kj >= 0)
        outs = []
        for kh in range(N_KV_HEADS):
            kk = k2[:, kh * HEAD_DIM:(kh + 1) * HEAD_DIM]
            vv = v2[:, kh * HEAD_DIM:(kh + 1) * HEAD_DIM]
            q4 = jnp.concatenate(
                [q[:, (kh * GQA_GROUP + g) * HEAD_DIM:(kh * GQA_GROUP + g + 1) * HEAD_DIM]
                 for g in range(GQA_GROUP)], axis=0)
            sk = jnp.concatenate(
                [jnp.full((WINDOW, 1), 1.0, F32) * sink_ref[kh * GQA_GROUP + g] for g in range(GQA_GROUP)], axis=0)
            s = _dot_nt(q4, kk) * ATTN_SCALE
            s = jnp.where(mask, s, MASK_NEG)
            m = jnp.maximum(jnp.max(s, axis=-1, keepdims=True), sk)
            p = jnp.exp(s - m)
            p = p / (jnp.sum(p, axis=-1, keepdims=True) + jnp.exp(sk - m))
            o4 = _dot(p.astype(BF16), vv)
            for g in range(GQA_GROUP):
                outs.append(o4[g * WINDOW:(g + 1) * WINDOW, :])
        o_blocks.append(jnp.concatenate(outs, axis=1))
    o = jnp.concatenate(o_blocks, axis=0)
    y_ref[0] = x_ref[0] + _dot(o.astype(BF16), wo_ref[...])


def _attn_prompt(x, q, k, v, sinks, wo, nsub=4):
    b, s, _ = x.shape
    tq = nsub * WINDOW
    cur = lambda i, j: (i, j, 0)
    prev = lambda i, j: (i, jnp.maximum(j * nsub - 1, 0), 0)
    return pl.pallas_call(
        functools.partial(_attn_prompt_kernel, nsub=nsub),
        grid=(b, s // tq),
        in_specs=[
            pl.BlockSpec((1, tq, D_MODEL), cur),
            pl.BlockSpec((1, tq, Q_WIDTH), cur),
            pl.BlockSpec((1, WINDOW, KV_WIDTH), prev),
            pl.BlockSpec((1, tq, KV_WIDTH), cur),
            pl.BlockSpec((1, WINDOW, KV_WIDTH), prev),
            pl.BlockSpec((1, tq, KV_WIDTH), cur),
            pl.BlockSpec(memory_space=pltpu.SMEM),
            pl.BlockSpec((Q_WIDTH, D_MODEL), lambda i, j: (0, 0)),
        ],
        out_specs=pl.BlockSpec((1, tq, D_MODEL), cur),
        out_shape=jax.ShapeDtypeStruct((b, s, D_MODEL), F32),
        compiler_params=_cparams(2),
        name="attn_prompt",
    )(x, q, k, k, v, v, sinks, wo)


def _attn_sample_kernel(x_ref, q_ref, kn_ref, vn_ref, ck_ref, cv_ref, sink_ref, wo_ref,
                        y_ref, nk_ref, nv_ref, *, bb, t):
    q = q_ref[...].reshape(bb, t, Q_WIDTH)
    kn = kn_ref[...].reshape(bb, t, KV_WIDTH)
    vn = vn_ref[...].reshape(bb, t, KV_WIDTH)
    ck = ck_ref[...]
    cv = cv_ref[...]
    nk_ref[:, 0:WINDOW - t, :] = ck[:, t:, :]
    nk_ref[:, WINDOW - t:, :] = kn
    nv_ref[:, 0:WINDOW - t, :] = cv[:, t:, :]
    nv_ref[:, WINDOW - t:, :] = vn
    rows = GQA_GROUP * t
    qi_c = lax.broadcasted_iota(jnp.int32, (bb, rows, WINDOW), 1) % t
    kj_c = lax.broadcasted_iota(jnp.int32, (bb, rows, WINDOW), 2)
    mask_c = kj_c > qi_c
    qi_n = lax.broadcasted_iota(jnp.int32, (bb, rows, t), 1) % t
    kj_n = lax.broadcasted_iota(jnp.int32, (bb, rows, t), 2)
    mask_n = kj_n <= qi_n
    ckb, cvb, knb, vnb = ck.astype(BF16), cv.astype(BF16), kn.astype(BF16), vn.astype(BF16)
    outs = []
    for kh in range(N_KV_HEADS):
        hs = slice(kh * HEAD_DIM, (kh + 1) * HEAD_DIM)
        q4 = jnp.concatenate(
            [q[:, :, (kh * GQA_GROUP + g) * HEAD_DIM:(kh * GQA_GROUP + g + 1) * HEAD_DIM] for g in range(GQA_GROUP)],
            axis=1).astype(BF16)
        sk = jnp.concatenate(
            [jnp.full((1, t, 1), 1.0, F32) * sink_ref[kh * GQA_GROUP + g] for g in range(GQA_GROUP)], axis=1)
        s_c = jnp.einsum("bqd,bkd->bqk", q4, ckb[:, :, hs], preferred_element_type=F32) * ATTN_SCALE
        s_n = jnp.einsum("bqd,bkd->bqk", q4, knb[:, :, hs], preferred_element_type=F32) * ATTN_SCALE
        s_c = jnp.where(mask_c, s_c, MASK_NEG)
        s_n = jnp.where(mask_n, s_n, MASK_NEG)
        m = jnp.maximum(jnp.maximum(jnp.max(s_c, axis=-1, keepdims=True),
                                    jnp.max(s_n, axis=-1, keepdims=True)), sk)
        p_c = jnp.exp(s_c - m)
        p_n = jnp.exp(s_n - m)
        den = jnp.sum(p_c, axis=-1, keepdims=True) + jnp.sum(p_n, axis=-1, keepdims=True) + jnp.exp(sk - m)
        inv = 1.0 / den
        o4 = (jnp.einsum("bqk,bkd->bqd", (p_c * inv).astype(BF16), cvb[:, :, hs], preferred_element_type=F32)
              + jnp.einsum("bqk,bkd->bqd", (p_n * inv).astype(BF16), vnb[:, :, hs], preferred_element_type=F32))
        for g in range(GQA_GROUP):
            outs.append(o4[:, g * t:(g + 1) * t, :])
    o = jnp.concatenate(outs, axis=2).reshape(bb * t, Q_WIDTH)
    y_ref[...] = x_ref[...] + _dot(o.astype(BF16), wo_ref[...])


def _attn_sample(x2, q, kn, vn, ck, cv, sinks, wo, t, bb=16):
    n = x2.shape[0]
    nb = n // t
    row = lambda i: (i, 0)
    cache = lambda i: (i, 0, 0)
    return pl.pallas_call(
        functools.partial(_attn_sample_kernel, bb=bb, t=t),
        grid=(nb // bb,),
        in_specs=[
            pl.BlockSpec((bb * t, D_MODEL), row),
            pl.BlockSpec((bb * t, Q_WIDTH), row),
            pl.BlockSpec((bb * t, KV_WIDTH), row),
            pl.BlockSpec((bb * t, KV_WIDTH), row),
            pl.BlockSpec((bb, WINDOW, KV_WIDTH), cache),
            pl.BlockSpec((bb, WINDOW, KV_WIDTH), cache),
            pl.BlockSpec(memory_space=pltpu.SMEM),
            pl.BlockSpec((Q_WIDTH, D_MODEL), lambda i: (0, 0)),
        ],
        out_specs=[
            pl.BlockSpec((bb * t, D_MODEL), row),
            pl.BlockSpec((bb, WINDOW, KV_WIDTH), cache),
            pl.BlockSpec((bb, WINDOW, KV_WIDTH), cache),
        ],
        out_shape=[
            jax.ShapeDtypeStruct((n, D_MODEL), F32),
            jax.ShapeDtypeStruct((nb, WINDOW, KV_WIDTH), F32),
            jax.ShapeDtypeStruct((nb, WINDOW, KV_WIDTH), F32),
        ],
        compiler_params=_cparams(1),
        name="attn_sample",
    )(x2, q, kn, vn, ck, cv, sinks, wo)


_CAND_LIMITS = tuple(PEER_TOPK // (r2 + 1) for r2 in range(1, SUBLANES))


def _sorting_network(n):
    pairs = []
    p = 1
    while p < n:
        k = p
        while k >= 1:
            for j in range(k % p, n - k, 2 * k):
                for i in range(min(k, n - j - k)):
                    if (i + j) // (2 * p) == (i + j + k) // (2 * p):
                        pairs.append((i + j, i + j + k))
            k //= 2
        p *= 2
    return tuple(pairs)


_SORT_PAIRS = _sorting_network(PEER_TOPK)


def _compare_exchange(x, i, j):
    if x[j] is not None:
        x[i], x[j] = jnp.maximum(x[i], x[j]), jnp.minimum(x[i], x[j])


def _top_sorted(tiles):
    n = len(tiles)
    x = list(tiles) + [None] * (PEER_TOPK - n)
    for i, j in _SORT_PAIRS:
        if j < n:
            _compare_exchange(x, i, j)
    for shift in (4, 2, 1):
        y = [None if v is None else pltpu.roll(v, shift, 0) for v in x]
        m = []
        for k in range(PEER_TOPK):
            a, b = x[k], y[PEER_TOPK - 1 - k]
            m.append(b if a is None else a if b is None else jnp.maximum(a, b))
        d = PEER_TOPK // 2
        while d:
            for k in range(PEER_TOPK):
                if not k & d:
                    _compare_exchange(m, k, k + d)
            d //= 2
        x = m
    return x


def _peer_select(s1, s2, cp_s):
    nt8 = PEER_NKEYS // SUBLANES
    s1 = s1.reshape(nt8, SUBLANES, LANES)
    s2 = s2.reshape(nt8, SUBLANES, LANES)
    v1 = _top_sorted([s1[i] for i in range(nt8)])
    v2 = _top_sorted([s2[i] for i in range(nt8)])
    row8 = lax.broadcasted_iota(jnp.int32, (SUBLANES, LANES), 0)

    def column(v, start):
        col = v[start]
        for s in range(1, SUBLANES):
            col = jnp.where(row8 == s, v[start + s], col)
        return col

    v1_lo, v1_hi, v2_hi = column(v1, 0), column(v1, SUBLANES), column(v2, SUBLANES)
    cand = [v1_lo + v2[0], v1_hi + v2[0]]
    for r2, lim in enumerate(_CAND_LIMITS, start=1):
        g = v1_lo + v2[r2]
        cand.append(jnp.where(row8 < lim, g, NEG_INF) if lim < SUBLANES else g)
    cand.append(v2_hi + v1[0])
    tau = _top_sorted(cand)[PEER_TOPK - 1]
    best = v1[0] + v2[0]
    sel = [c >= tau for c in cand]
    zt = jnp.zeros((SUBLANES, LANES), F32)
    for c, m in zip(cand, sel):
        zt = zt + jnp.where(m, jnp.exp(c - best), 0.0)
    z = jnp.sum(zt, axis=0, keepdims=True)
    self32 = [m.astype(F32) for m in sel]
    lo = self32[0]
    for t in self32[2:-1]:
        lo = lo + t
    tail = jnp.sum(self32[-1], axis=0, keepdims=True)
    cp_s[0:SUBLANES, :] = lo + jnp.where(row8 == 0, tail, 0.0)
    cp_s[SUBLANES:, :] = self32[1]
    count1 = jnp.zeros_like(s1)
    rank2 = jnp.zeros_like(s2)
    for r in range(PEER_TOPK):
        count1 = jnp.maximum(count1, jnp.where(s1 >= v1[r], cp_s[r:r + 1, :], 0.0))
        rank2 = rank2 + jnp.where(v2[r] > s2, 1.0, 0.0)
    e2 = jnp.exp(s2 - v2[0])
    e1z = jnp.exp(s1 - v1[0]) * (1.0 / z)
    return rank2, e2, count1, e1z


def _peer_kernel(x_ref, g_ref, wq_ref, sk_ref, u_ref, v_ref, o_ref,
                 hn_s, s_s, cp_s, r2_s, e2_s, c1_s, e1_s, a0_s, a1_s, w_s, y_s,
                 *, tm, te, tc, kb, nt):
    j = pl.program_id(1)
    nj = pl.num_programs(1)
    nlc = tm // LANES
    gi = te // PEER_NKEYS
    packed = (PEER_NKEYS // GATE_ROWS, GATE_ROWS, LANES)

    @pl.when(j == 0)
    def _():
        hn = _rmsnorm(x_ref[...], g_ref[...]).astype(BF16)
        hn_s[...] = hn
        for h in range(PEER_HEADS):
            qh = _dot(hn, wq_ref[:, h * PEER_DKEY:(h + 1) * PEER_DKEY]).astype(BF16)
            for p in range(2):
                half = PEER_DKEY // 2
                st = _dot_nt(sk_ref[h, p], qh[:, p * half:(p + 1) * half])
                for ck in range(nlc):
                    s_s[2 * h + p, ck] = st[:, ck * LANES:(ck + 1) * LANES]

        def select(idx, carry):
            h = idx // nlc
            ck = idx % nlc
            rank2, e2, count1, e1z = _peer_select(s_s[2 * h, ck], s_s[2 * h + 1, ck], cp_s)
            flat = (PEER_NKEYS, LANES)
            r2_s[h, ck] = rank2.reshape(flat).astype(GATE_DTYPE).reshape(packed)
            e2_s[h, ck] = e2.reshape(flat).astype(GATE_DTYPE).reshape(packed)
            c1_s[h, ck] = count1.reshape(flat)
            e1_s[h, ck] = e1z.reshape(flat)
            return carry

        lax.fori_loop(0, PEER_HEADS * nlc, select, 0)
        y_s[...] = jnp.zeros_like(y_s)
        a1_s[...] = jnp.zeros_like(a1_s)

    i1base = jnp.maximum(j - 1, 0) * gi
    per_kb = kb // PEER_NKEYS

    def step(a_w, a_r):
        for c in range(tm // tc):
            tok = slice(c * tc, (c + 1) * tc)
            a_w[c] = _dot_nt(u_ref[...], hn_s[tok, :])
        for b in range(te // kb):
            grp, l0 = divmod(b * per_kb, SUBLANES)
            i1s = pl.ds(pl.multiple_of(i1base + grp * SUBLANES, SUBLANES), SUBLANES)
            for ck in range(nlc):
                c, lc = divmod(ck, tc // LANES)
                ls = slice(lc * LANES, (lc + 1) * LANES)
                cnt = [c1_s[h, ck, i1s, :] for h in range(PEER_HEADS)]
                e1 = [e1_s[h, ck, i1s, :] for h in range(PEER_HEADS)]
                rows = range(l0, l0 + per_kb)
                accs = [jnp.zeros(packed, GATE_DTYPE) for _ in rows]
                for h in range(PEER_HEADS):
                    r2 = r2_s[h, ck]
                    e2 = e2_s[h, ck]
                    for n, l in enumerate(rows):
                        cb = jnp.broadcast_to(cnt[h][l:l + 1], (GATE_ROWS, LANES)).astype(GATE_DTYPE)
                        eb = jnp.broadcast_to(e1[h][l:l + 1], (GATE_ROWS, LANES)).astype(GATE_DTYPE)
                        accs[n] = accs[n] + jnp.where(r2 < cb, e2, 0.0) * eb
                for acc, l in zip(accs, rows):
                    rs = slice((grp * SUBLANES + l) * PEER_NKEYS, (grp * SUBLANES + l + 1) * PEER_NKEYS)
                    a = a_r[c, rs, ls].reshape(packed)
                    gelu = 0.5 * a * (1.0 + lax.erf(a * math.sqrt(0.5)))
                    w = (acc * gelu.astype(GATE_DTYPE)).reshape(PEER_NKEYS, LANES).astype(BF16)
                    w_s[ck * LANES:(ck + 1) * LANES, rs] = w.T
            es = slice(b * kb, (b + 1) * kb)
            y_s[...] += _dot(w_s[:, es], v_ref[es, :])

    @pl.when(j % 2 == 0)
    def _():
        step(a0_s, a1_s)

    @pl.when(j % 2 == 1)
    def _():
        step(a1_s, a0_s)

    @pl.when(j == nj - 1)
    def _():
        o_ref[...] = x_ref[...] + y_s[...]


def _peer(x2, g, wq, sk, u_tab, v_tab, tm=512, te=1024, tc=256, kb=256):
    n = x2.shape[0]
    nt = u_tab.shape[0] // te
    nlc = tm // LANES
    packed = (PEER_NKEYS // GATE_ROWS, GATE_ROWS, LANES)
    return pl.pallas_call(
        functools.partial(_peer_kernel, tm=tm, te=te, tc=tc, kb=kb, nt=nt),
        grid=(n // tm, nt + 1),
        in_specs=[
            pl.BlockSpec((tm, D_MODEL), lambda i, j: (i, 0)),
            pl.BlockSpec((1, D_MODEL), lambda i, j: (0, 0)),
            pl.BlockSpec(wq.shape, lambda i, j: (0, 0)),
            pl.BlockSpec(sk.shape, lambda i, j: (0, 0, 0, 0)),
            pl.BlockSpec((te, D_MODEL), lambda i, j: (jnp.minimum(j, nt - 1), 0)),
            pl.BlockSpec((te, D_MODEL), lambda i, j: (jnp.maximum(j - 1, 0), 0)),
        ],
        out_specs=pl.BlockSpec((tm, D_MODEL), lambda i, j: (i, 0)),
        out_shape=jax.ShapeDtypeStruct((n, D_MODEL), F32),
        scratch_shapes=[
            pltpu.VMEM((tm, D_MODEL), BF16),
            pltpu.VMEM((2 * PEER_HEADS, nlc, PEER_NKEYS, LANES), F32),
            pltpu.VMEM((PEER_TOPK, LANES), F32),
            pltpu.VMEM((PEER_HEADS, nlc) + packed, GATE_DTYPE),
            pltpu.VMEM((PEER_HEADS, nlc) + packed, GATE_DTYPE),
            pltpu.VMEM((PEER_HEADS, nlc, PEER_NKEYS, LANES), F32),
            pltpu.VMEM((PEER_HEADS, nlc, PEER_NKEYS, LANES), F32),
            pltpu.VMEM((tm // tc, te, tc), F32),
            pltpu.VMEM((tm // tc, te, tc), F32),
            pltpu.VMEM((tm, te), BF16),
            pltpu.VMEM((tm, D_MODEL), F32),
        ],
        compiler_params=_cparams(2),
        name="peer",
    )(x2, g, wq, sk, u_tab, v_tab)


def _rope_tables(pos):
    half = HEAD_DIM // 2
    inv_freq = jnp.power(jnp.float32(ROPE_THETA), -jnp.arange(half, dtype=F32) * (2.0 / HEAD_DIM))
    ang = pos.astype(F32)[:, None] * inv_freq[None, :]
    cos = jnp.concatenate([jnp.cos(ang)] * 4, axis=1)
    sin = jnp.concatenate([jnp.sin(ang)] * 4, axis=1)
    return cos, sin


def kernel(x_prompt, x_sample, state_pool, cache_k, cache_v, norm_mix, norm_ffn, pool_w_in, pool_w_group,
           pool_scale, attn_w_qkv, attn_q_norm, attn_k_norm, attn_sinks, attn_w_o, peer_w_q, peer_subkeys,
           peer_u, peer_v):
    b, s, _ = x_prompt.shape
    db, t, _ = x_sample.shape
    past_len = 16384
    depth = norm_mix.shape[0]
    xp = x_prompt
    xs = x_sample.reshape(db * t, D_MODEL)
    seg = jnp.asarray(np.kron(np.eye(256 // HEAD_DIM), np.ones((HEAD_DIM, HEAD_DIM))), BF16)
    cos_p, sin_p = _rope_tables(jnp.arange(s, dtype=jnp.int32))
    cos_s, sin_s = _rope_tables(past_len + jnp.arange(t, dtype=jnp.int32))
    pool_p, pool_s, kp_l, vp_l, ks_l, vs_l = [], [], [], [], [], []
    for i in range(depth):
        jl = i // 2
        g_mix = norm_mix[i][None, :]
        if i % 2 == 0:
            win = pool_w_in[jl].astype(BF16)
            wg = pool_w_group[jl].astype(BF16)
            sc = pool_scale[jl][None, :]
            xp, st_p = _pool_prompt(xp, g_mix, win, wg, sc)
            prior16 = jnp.pad(state_pool[jl], ((0, 0), (1, 0), (0, 0)))
            xs, st_s = _pool_sample(xs, prior16, g_mix, win, wg, sc, t)
            pool_p.append(st_p[:, 1:])
            pool_s.append(st_s[:, 1:])
        else:
            wqkv = attn_w_qkv[jl].astype(BF16)
            wo = attn_w_o[jl].astype(BF16)
            gain = jnp.concatenate([jnp.tile(attn_q_norm[jl], N_HEADS), jnp.tile(attn_k_norm[jl], N_KV_HEADS)])[None, :]
            sinks = attn_sinks[jl]
            tq = 512
            qp, kp, vp = _qkv(xp.reshape(b * s, D_MODEL), g_mix, wqkv, seg, gain, cos_p, sin_p, tq, s // tq)
            kp3 = kp.reshape(b, s, KV_WIDTH)
            vp3 = vp.reshape(b, s, KV_WIDTH)
            xp = _attn_prompt(xp, qp.reshape(b, s, Q_WIDTH), kp3, vp3, sinks, wo)
            cos_st = jnp.tile(cos_s, (tq // t, 1))
            sin_st = jnp.tile(sin_s, (tq // t, 1))
            qs, kq, vq = _qkv(xs, g_mix, wqkv, seg, gain, cos_st, sin_st, tq, 1)
            ck = cache_k[jl].reshape(db, WINDOW, KV_WIDTH)
            cv = cache_v[jl].reshape(db, WINDOW, KV_WIDTH)
            xs, nk_s, nv_s = _attn_sample(xs, qs, kq, vq, ck, cv, sinks, wo, t)
            kp_l.append(kp3[:, -WINDOW:].reshape(b, WINDOW, N_KV_HEADS, HEAD_DIM))
            vp_l.append(vp3[:, -WINDOW:].reshape(b, WINDOW, N_KV_HEADS, HEAD_DIM))
            ks_l.append(nk_s.reshape(db, WINDOW, N_KV_HEADS, HEAD_DIM))
            vs_l.append(nv_s.reshape(db, WINDOW, N_KV_HEADS, HEAD_DIM))
        g_ffn = norm_ffn[i][None, :]
        wq = peer_w_q[i].astype(BF16)
        sk = peer_subkeys[i].astype(BF16)
        u_tab = peer_u[i].astype(BF16)
        v_tab = peer_v[i].astype(BF16)
        xp = _peer(xp.reshape(b * s, D_MODEL), g_ffn, wq, sk, u_tab, v_tab).reshape(b, s, D_MODEL)
        xs = _peer(xs, g_ffn, wq, sk, u_tab, v_tab)
    return (xp, xs.reshape(db, t, D_MODEL), jnp.stack(pool_p), jnp.stack(pool_s), jnp.stack(kp_l),
            jnp.stack(vp_l), jnp.stack(ks_l), jnp.stack(vs_l))
```

```python
import functools
import math

import jax
import jax.numpy as jnp
import numpy as np
from jax import lax
from jax.experimental import pallas as pl
from jax.experimental.pallas import tpu as pltpu

F32 = jnp.float32
BF16 = jnp.bfloat16

D_MODEL = 1024
LANES = 128
SUBLANES = 8
GATE_DTYPE = jnp.float32
GATE_ROWS = 8
NORM_EPS = 1e-6
NEG_INF = float("-inf")

POOL_WINDOWS = (2, 4, 8, 16)
POOL_GROUP_DIM = D_MODEL // len(POOL_WINDOWS)
POOL_HIST = 16

HEAD_DIM = 64
N_HEADS = 16
N_KV_HEADS = 4
GQA_GROUP = N_HEADS // N_KV_HEADS
WINDOW = 128
ROPE_THETA = 10000.0
ATTN_SCALE = 1.0 / math.sqrt(HEAD_DIM)
Q_WIDTH = N_HEADS * HEAD_DIM
KV_WIDTH = N_KV_HEADS * HEAD_DIM
MASK_NEG = -1e30

PEER_HEADS = 8
PEER_NKEYS = 128
PEER_TOPK = 16
PEER_DKEY = 256

VMEM_LIMIT = 56 * 1024 * 1024


def _cparams(n_axes, **kw):
    return pltpu.CompilerParams(dimension_semantics=("arbitrary",) * n_axes,
                                vmem_limit_bytes=VMEM_LIMIT, **kw)


def _rmsnorm(x, g):
    return x * lax.rsqrt(jnp.mean(x * x, axis=-1, keepdims=True) + NORM_EPS) * g


def _dot(a, b):
    return jnp.dot(a, b, preferred_element_type=F32)


def _dot_nt(a, b):
    return lax.dot_general(a, b, (((1,), (1,)), ((), ())), preferred_element_type=F32)


def _dot_tn(a, b):
    return lax.dot_general(a, b, (((0,), (0,)), ((), ())), preferred_element_type=F32)


def _pool_prompt_kernel(x_ref, g_ref, win_ref, wg_ref, sc_ref, y_ref, st_ref, ext_s, *, ts):
    sb = pl.program_id(1)

    @pl.when(sb == 0)
    def _():
        ext_s[0:POOL_HIST, :] = jnp.zeros((POOL_HIST, D_MODEL), F32)

    x = x_ref[0]
    h = _rmsnorm(x, g_ref[...])
    u = _dot(h.astype(BF16), win_ref[...])
    ext_s[POOL_HIST:POOL_HIST + ts, :] = u
    pos = sb * ts + lax.broadcasted_iota(jnp.int32, (ts, 1), 0)
    outs = []
    for g, w in enumerate(POOL_WINDOWS):
        c0 = g * POOL_GROUP_DIM
        ug = u[:, c0:c0 + POOL_GROUP_DIM]
        win_sum = ug
        for j in range(1, w):
            win_sum = win_sum + ext_s[POOL_HIST - j:POOL_HIST - j + ts, c0:c0 + POOL_GROUP_DIM]
        count = jnp.minimum(pos + 1, w).astype(F32)
        pooled = win_sum / count - ug
        outs.append(_dot(pooled.astype(BF16), wg_ref[g]))
    y = jnp.concatenate(outs, axis=1) * sc_ref[...]
    y_ref[0] = x + y
    st_ref[0] = u[ts - POOL_HIST:ts, :]
    ext_s[0:POOL_HIST, :] = u[ts - POOL_HIST:ts, :]


def _pool_prompt(x, g, win, wg, sc, ts=512):
    b, s, _ = x.shape
    return pl.pallas_call(
        functools.partial(_pool_prompt_kernel, ts=ts),
        grid=(b, s // ts),
        in_specs=[
            pl.BlockSpec((1, ts, D_MODEL), lambda i, j: (i, j, 0)),
            pl.BlockSpec((1, D_MODEL), lambda i, j: (0, 0)),
            pl.BlockSpec((D_MODEL, D_MODEL), lambda i, j: (0, 0)),
            pl.BlockSpec((len(POOL_WINDOWS), POOL_GROUP_DIM, POOL_GROUP_DIM), lambda i, j: (0, 0, 0)),
            pl.BlockSpec((1, D_MODEL), lambda i, j: (0, 0)),
        ],
        out_specs=[
            pl.BlockSpec((1, ts, D_MODEL), lambda i, j: (i, j, 0)),
            pl.BlockSpec((1, POOL_HIST, D_MODEL), lambda i, j: (i, 0, 0)),
        ],
        out_shape=[
            jax.ShapeDtypeStruct((b, s, D_MODEL), F32),
            jax.ShapeDtypeStruct((b, POOL_HIST, D_MODEL), F32),
        ],
        scratch_shapes=[pltpu.VMEM((POOL_HIST + ts, D_MODEL), F32)],
        compiler_params=_cparams(2),
        name="pool_prompt",
    )(x, g, win, wg, sc)


def _pool_sample_kernel(x_ref, pr_ref, g_ref, win_ref, wg_ref, sc_ref, y_ref, st_ref, ext_s, *, bb, t):
    x = x_ref[...]
    h = _rmsnorm(x, g_ref[...])
    u = _dot(h.astype(BF16), win_ref[...])
    ext_s[:, 0:POOL_HIST, :] = pr_ref[...]
    ext_s[:, POOL_HIST:POOL_HIST + t, :] = u.reshape(bb, t, D_MODEL)
    outs = []
    for g, w in enumerate(POOL_WINDOWS):
        c0 = g * POOL_GROUP_DIM
        win_sum = ext_s[:, POOL_HIST:POOL_HIST + t, c0:c0 + POOL_GROUP_DIM]
        for j in range(1, w):
            win_sum = win_sum + ext_s[:, POOL_HIST - j:POOL_HIST - j + t, c0:c0 + POOL_GROUP_DIM]
        pooled = win_sum.reshape(bb * t, POOL_GROUP_DIM) / float(w) - u[:, c0:c0 + POOL_GROUP_DIM]
        outs.append(_dot(pooled.astype(BF16), wg_ref[g]))
    y = jnp.concatenate(outs, axis=1) * sc_ref[...]
    y_ref[...] = x + y
    st_ref[...] = ext_s[:, t:t + POOL_HIST, :]


def _pool_sample(x2, prior16, g, win, wg, sc, t, bb=32):
    n = x2.shape[0]
    nb = n // t
    return pl.pallas_call(
        functools.partial(_pool_sample_kernel, bb=bb, t=t),
        grid=(nb // bb,),
        in_specs=[
            pl.BlockSpec((bb * t, D_MODEL), lambda i: (i, 0)),
            pl.BlockSpec((bb, POOL_HIST, D_MODEL), lambda i: (i, 0, 0)),
            pl.BlockSpec((1, D_MODEL), lambda i: (0, 0)),
            pl.BlockSpec((D_MODEL, D_MODEL), lambda i: (0, 0)),
            pl.BlockSpec((len(POOL_WINDOWS), POOL_GROUP_DIM, POOL_GROUP_DIM), lambda i: (0, 0, 0)),
            pl.BlockSpec((1, D_MODEL), lambda i: (0, 0)),
        ],
        out_specs=[
            pl.BlockSpec((bb * t, D_MODEL), lambda i: (i, 0)),
            pl.BlockSpec((bb, POOL_HIST, D_MODEL), lambda i: (i, 0, 0)),
        ],
        out_shape=[
            jax.ShapeDtypeStruct((n, D_MODEL), F32),
            jax.ShapeDtypeStruct((nb, POOL_HIST, D_MODEL), F32),
        ],
        scratch_shapes=[pltpu.VMEM((bb, POOL_HIST + t, D_MODEL), F32)],
        compiler_params=_cparams(1),
        name="pool_sample",
    )(x2, prior16, g, win, wg, sc)


def _qkv_kernel(x_ref, g_ref, w_ref, seg_ref, gain_ref, cos_ref, sin_ref, q_ref, k_ref, v_ref):
    x = x_ref[...]
    h = _rmsnorm(x, g_ref[...])
    qkv = _dot(h.astype(BF16), w_ref[...])
    nqk = Q_WIDTH + KV_WIDTH
    qk = qkv[:, :nqk]
    sq = qk * qk
    sq_hi = sq.astype(BF16)
    sq_lo = (sq - sq_hi.astype(F32)).astype(BF16)
    seg = seg_ref[...]
    cw = seg.shape[0]
    parts = []
    for c in range(nqk // cw):
        sl = slice(c * cw, (c + 1) * cw)
        parts.append(_dot(sq_hi[:, sl], seg) + _dot(sq_lo[:, sl], seg))
    ssq = jnp.concatenate(parts, axis=1)
    qk = qk * lax.rsqrt(ssq * (1.0 / HEAD_DIM) + NORM_EPS) * gain_ref[...]
    half = HEAD_DIM // 2
    lane = lax.broadcasted_iota(jnp.int32, qk.shape, 1)
    first = (lane % HEAD_DIM) < half
    up = pltpu.roll(qk, nqk - half, 1)
    down = pltpu.roll(qk, half, 1)
    rot = jnp.where(first, -up, down)
    reps = nqk // cos_ref.shape[1]
    cos = jnp.concatenate([cos_ref[...]] * reps, axis=1)
    sin = jnp.concatenate([sin_ref[...]] * reps, axis=1)
    qk = qk * cos + rot * sin
    q_ref[...] = qk[:, :Q_WIDTH]
    k_ref[...] = qk[:, Q_WIDTH:]
    v_ref[...] = qkv[:, nqk:]


def _qkv(x2, g, w, seg, gain, cos, sin, tq, pos_blocks):
    n = x2.shape[0]
    tw = cos.shape[1]
    return pl.pallas_call(
        _qkv_kernel,
        grid=(n // tq,),
        in_specs=[
            pl.BlockSpec((tq, D_MODEL), lambda i: (i, 0)),
            pl.BlockSpec((1, D_MODEL), lambda i: (0, 0)),
            pl.BlockSpec(w.shape, lambda i: (0, 0)),
            pl.BlockSpec(seg.shape, lambda i: (0, 0)),
            pl.BlockSpec((1, Q_WIDTH + KV_WIDTH), lambda i: (0, 0)),
            pl.BlockSpec((tq, tw), lambda i: (i % pos_blocks, 0)),
            pl.BlockSpec((tq, tw), lambda i: (i % pos_blocks, 0)),
        ],
        out_specs=[
            pl.BlockSpec((tq, Q_WIDTH), lambda i: (i, 0)),
            pl.BlockSpec((tq, KV_WIDTH), lambda i: (i, 0)),
            pl.BlockSpec((tq, KV_WIDTH), lambda i: (i, 0)),
        ],
        out_shape=[
            jax.ShapeDtypeStruct((n, Q_WIDTH), F32),
            jax.ShapeDtypeStruct((n, KV_WIDTH), F32),
            jax.ShapeDtypeStruct((n, KV_WIDTH), F32),
        ],
        compiler_params=_cparams(1),
        name="attn_qkv",
    )(x2, g, w, seg, gain, cos, sin)


def _attn_prompt_kernel(x_ref, q_ref, kp_ref, kc_ref, vp_ref, vc_ref, sink_ref, wo_ref, y_ref, *, nsub):
    nb0 = pl.program_id(1) * nsub
    q_all = q_ref[0].astype(BF16)
    k_all = jnp.concatenate([kp_ref[0], kc_ref[0]], axis=0).astype(BF16)
    v_all = jnp.concatenate([vp_ref[0], vc_ref[0]], axis=0).astype(BF16)
    rows = GQA_GROUP * WINDOW
    qi = lax.broadcasted_iota(jnp.int32, (rows, 2 * WINDOW), 0) % WINDOW
    kj = lax.broadcasted_iota(jnp.int32, (rows, 2 * WINDOW), 1)
    band = (kj > qi) & (kj <= qi + WINDOW)
    o_blocks = []
    for sub in range(nsub):
        q = q_all[sub * WINDOW:(sub + 1) * WINDOW]
        k2 = k_all[sub * WINDOW:(sub + 2) * WINDOW]
        v2 = v_all[sub * WINDOW:(sub + 2) * WINDOW]
        mask = band & ((nb0 + sub) * WINDOW - WINDOW + kj >= 0)
        outs = []
        for kh in range(N_KV_HEADS):
            kk = k2[:, kh * HEAD_DIM:(kh + 1) * HEAD_DIM]
            vv = v2[:, kh * HEAD_DIM:(kh + 1) * HEAD_DIM]
            q4 = jnp.concatenate(
                [q[:, (kh * GQA_GROUP + g) * HEAD_DIM:(kh * GQA_GROUP + g + 1) * HEAD_DIM]
                 for g in range(GQA_GROUP)], axis=0)
            sk = jnp.concatenate(
                [jnp.full((WINDOW, 1), 1.0, F32) * sink_ref[kh * GQA_GROUP + g] for g in range(GQA_GROUP)], axis=0)
            s = _dot_nt(q4, kk) * ATTN_SCALE
            s = jnp.where(mask, s, MASK_NEG)
            m = jnp.maximum(jnp.max(s, axis=-1, keepdims=True), sk)
            p = jnp.exp(s - m)
            p = p / (jnp.sum(p, axis=-1, keepdims=True) + jnp.exp(sk - m))
            o4 = _dot(p.astype(BF16), vv)
            for g in range(GQA_GROUP):
                outs.append(o4[g * WINDOW:(g + 1) * WINDOW, :])
        o_blocks.append(jnp.concatenate(outs, axis=1))
    o = jnp.concatenate(o_blocks, axis=0)
    y_ref[0] = x_ref[0] + _dot(o.astype(BF16), wo_ref[...])


def _attn_prompt(x, q, k, v, sinks, wo, nsub=4):
    b, s, _ = x.shape
    tq = nsub * WINDOW
    cur = lambda i, j: (i, j, 0)
    prev = lambda i, j: (i, jnp.maximum(j * nsub - 1, 0), 0)
    return pl.pallas_call(
        functools.partial(_attn_prompt_kernel, nsub=nsub),
        grid=(b, s // tq),
        in_specs=[
            pl.BlockSpec((1, tq, D_MODEL), cur),
            pl.BlockSpec((1, tq, Q_WIDTH), cur),
            pl.BlockSpec((1, WINDOW, KV_WIDTH), prev),
            pl.BlockSpec((1, tq, KV_WIDTH), cur),
            pl.BlockSpec((1, WINDOW, KV_WIDTH), prev),
            pl.BlockSpec((1, tq, KV_WIDTH), cur),
            pl.BlockSpec(memory_space=pltpu.SMEM),
            pl.BlockSpec((Q_WIDTH, D_MODEL), lambda i, j: (0, 0)),
        ],
        out_specs=pl.BlockSpec((1, tq, D_MODEL), cur),
        out_shape=jax.ShapeDtypeStruct((b, s, D_MODEL), F32),
        compiler_params=_cparams(2),
        name="attn_prompt",
    )(x, q, k, k, v, v, sinks, wo)


def _attn_sample_kernel(x_ref, q_ref, kn_ref, vn_ref, ck_ref, cv_ref, sink_ref, wo_ref,
                        y_ref, nk_ref, nv_ref, *, bb, t):
    q = q_ref[...].reshape(bb, t, Q_WIDTH)
    kn = kn_ref[...].reshape(bb, t, KV_WIDTH)
    vn = vn_ref[...].reshape(bb, t, KV_WIDTH)
    ck = ck_ref[...]
    cv = cv_ref[...]
    nk_ref[:, 0:WINDOW - t, :] = ck[:, t:, :]
    nk_ref[:, WINDOW - t:, :] = kn
    nv_ref[:, 0:WINDOW - t, :] = cv[:, t:, :]
    nv_ref[:, WINDOW - t:, :] = vn
    rows = GQA_GROUP * t
    qi_c = lax.broadcasted_iota(jnp.int32, (bb, rows, WINDOW), 1) % t
    kj_c = lax.broadcasted_iota(jnp.int32, (bb, rows, WINDOW), 2)
    mask_c = kj_c > qi_c
    qi_n = lax.broadcasted_iota(jnp.int32, (bb, rows, t), 1) % t
    kj_n = lax.broadcasted_iota(jnp.int32, (bb, rows, t), 2)
    mask_n = kj_n <= qi_n
    ckb, cvb, knb, vnb = ck.astype(BF16), cv.astype(BF16), kn.astype(BF16), vn.astype(BF16)
    outs = []
    for kh in range(N_KV_HEADS):
        hs = slice(kh * HEAD_DIM, (kh + 1) * HEAD_DIM)
        q4 = jnp.concatenate(
            [q[:, :, (kh * GQA_GROUP + g) * HEAD_DIM:(kh * GQA_GROUP + g + 1) * HEAD_DIM] for g in range(GQA_GROUP)],
            axis=1).astype(BF16)
        sk = jnp.concatenate(
            [jnp.full((1, t, 1), 1.0, F32) * sink_ref[kh * GQA_GROUP + g] for g in range(GQA_GROUP)], axis=1)
        s_c = jnp.einsum("bqd,bkd->bqk", q4, ckb[:, :, hs], preferred_element_type=F32) * ATTN_SCALE
        s_n = jnp.einsum("bqd,bkd->bqk", q4, knb[:, :, hs], preferred_element_type=F32) * ATTN_SCALE
        s_c = jnp.where(mask_c, s_c, MASK_NEG)
        s_n = jnp.where(mask_n, s_n, MASK_NEG)
        m = jnp.maximum(jnp.maximum(jnp.max(s_c, axis=-1, keepdims=True),
                                    jnp.max(s_n, axis=-1, keepdims=True)), sk)
        p_c = jnp.exp(s_c - m)
        p_n = jnp.exp(s_n - m)
        den = jnp.sum(p_c, axis=-1, keepdims=True) + jnp.sum(p_n, axis=-1, keepdims=True) + jnp.exp(sk - m)
        inv = 1.0 / den
        o4 = (jnp.einsum("bqk,bkd->bqd", (p_c * inv).astype(BF16), cvb[:, :, hs], preferred_element_type=F32)
              + jnp.einsum("bqk,bkd->bqd", (p_n * inv).astype(BF16), vnb[:, :, hs], preferred_element_type=F32))
        for g in range(GQA_GROUP):
            outs.append(o4[:, g * t:(g + 1) * t, :])
    o = jnp.concatenate(outs, axis=2).reshape(bb * t, Q_WIDTH)
    y_ref[...] = x_ref[...] + _dot(o.astype(BF16), wo_ref[...])


def _attn_sample(x2, q, kn, vn, ck, cv, sinks, wo, t, bb=16):
    n = x2.shape[0]
    nb = n // t
    row = lambda i: (i, 0)
    cache = lambda i: (i, 0, 0)
    return pl.pallas_call(
        functools.partial(_attn_sample_kernel, bb=bb, t=t),
        grid=(nb // bb,),
        in_specs=[
            pl.BlockSpec((bb * t, D_MODEL), row),
            pl.BlockSpec((bb * t, Q_WIDTH), row),
            pl.BlockSpec((bb * t, KV_WIDTH), row),
            pl.BlockSpec((bb * t, KV_WIDTH), row),
            pl.BlockSpec((bb, WINDOW, KV_WIDTH), cache),
            pl.BlockSpec((bb, WINDOW, KV_WIDTH), cache),
            pl.BlockSpec(memory_space=pltpu.SMEM),
            pl.BlockSpec((Q_WIDTH, D_MODEL), lambda i: (0, 0)),
        ],
        out_specs=[
            pl.BlockSpec((bb * t, D_MODEL), row),
            pl.BlockSpec((bb, WINDOW, KV_WIDTH), cache),
            pl.BlockSpec((bb, WINDOW, KV_WIDTH), cache),
        ],
        out_shape=[
            jax.ShapeDtypeStruct((n, D_MODEL), F32),
            jax.ShapeDtypeStruct((nb, WINDOW, KV_WIDTH), F32),
            jax.ShapeDtypeStruct((nb, WINDOW, KV_WIDTH), F32),
        ],
        compiler_params=_cparams(1),
        name="attn_sample",
    )(x2, q, kn, vn, ck, cv, sinks, wo)


_CAND_LIMITS = tuple(PEER_TOPK // (r2 + 1) for r2 in range(1, SUBLANES))


def _sorting_network(n):
    pairs = []
    p = 1
    while p < n:
        k = p
        while k >= 1:
            for j in range(k % p, n - k, 2 * k):
                for i in range(min(k, n - j - k)):
                    if (i + j) // (2 * p) == (i + j + k) // (2 * p):
                        pairs.append((i + j, i + j + k))
            k //= 2
        p *= 2
    return tuple(pairs)


_SORT_PAIRS = _sorting_network(PEER_TOPK)


def _compare_exchange(x, i, j):
    if x[j] is not None:
        x[i], x[j] = jnp.maximum(x[i], x[j]), jnp.minimum(x[i], x[j])


def _top_sorted(tiles):
    n = len(tiles)
    x = list(tiles) + [None] * (PEER_TOPK - n)
    for i, j in _SORT_PAIRS:
        if j < n:
            _compare_exchange(x, i, j)
    for shift in (4, 2, 1):
        y = [None if v is None else pltpu.roll(v, shift, 0) for v in x]
        m = []
        for k in range(PEER_TOPK):
            a, b = x[k], y[PEER_TOPK - 1 - k]
            m.append(b if a is None else a if b is None else jnp.maximum(a, b))
        d = PEER_TOPK // 2
        while d:
            for k in range(PEER_TOPK):
                if not k & d:
                    _compare_exchange(m, k, k + d)
            d //= 2
        x = m
    return x


def _rank_bits(v, s):
    b8 = v[7] > s
    b4 = jnp.where(b8, v[11], v[3]) > s
    b2 = jnp.where(b8, jnp.where(b4, v[13], v[9]), jnp.where(b4, v[5], v[1])) > s
    piv = [jnp.where(b2, v[4 * i + 2], v[4 * i]) for i in range(4)]
    piv = [jnp.where(b4, piv[2 * i + 1], piv[2 * i]) for i in range(2)]
    b1 = jnp.where(b8, piv[1], piv[0]) > s
    b0 = v[PEER_TOPK - 1] > s
    return b8, b4, b2, b1, b0


def _peer_select(s1, s2, cp_s):
    nt8 = PEER_NKEYS // SUBLANES
    s1 = s1.reshape(nt8, SUBLANES, LANES)
    s2 = s2.reshape(nt8, SUBLANES, LANES)
    v1 = _top_sorted([s1[i] for i in range(nt8)])
    v2 = _top_sorted([s2[i] for i in range(nt8)])
    row8 = lax.broadcasted_iota(jnp.int32, (SUBLANES, LANES), 0)

    def column(v, start):
        col = v[start]
        for s in range(1, SUBLANES):
            col = jnp.where(row8 == s, v[start + s], col)
        return col

    v1_lo, v1_hi, v2_hi = column(v1, 0), column(v1, SUBLANES), column(v2, SUBLANES)
    cand = [v1_lo + v2[0], v1_hi + v2[0]]
    for r2, lim in enumerate(_CAND_LIMITS, start=1):
        g = v1_lo + v2[r2]
        cand.append(jnp.where(row8 < lim, g, NEG_INF) if lim < SUBLANES else g)
    cand.append(v2_hi + v1[0])
    tau = _top_sorted(cand)[PEER_TOPK - 1]
    best = v1[0] + v2[0]
    sel = [c >= tau for c in cand]
    zt = jnp.zeros((SUBLANES, LANES), F32)
    for c, m in zip(cand, sel):
        zt = zt + jnp.where(m, jnp.exp(c - best), 0.0)
    z = jnp.sum(zt, axis=0, keepdims=True)
    self32 = [m.astype(F32) for m in sel]
    lo = self32[0]
    for t in self32[2:-1]:
        lo = lo + t
    tail = jnp.sum(self32[-1], axis=0, keepdims=True)
    cp_s[0:SUBLANES, :] = lo + jnp.where(row8 == 0, tail, 0.0)
    cp_s[SUBLANES:, :] = self32[1]
    bits1 = _rank_bits(v1, s1)
    bits2 = _rank_bits(v2, s2)
    rank2 = jnp.zeros_like(s2)
    for weight, bit in zip((8.0, 4.0, 2.0, 1.0, 1.0), bits2):
        rank2 = rank2 + jnp.where(bit, weight, 0.0)
    level = [cp_s[r:r + 1, :] for r in range(PEER_TOPK)]
    for bit in bits1[3::-1]:
        level = [jnp.where(bit, level[2 * i + 1], level[2 * i]) for i in range(len(level) // 2)]
    count1 = jnp.where(bits1[4], 0.0, level[0])
    e2 = jnp.exp(s2 - v2[0])
    e1z = jnp.exp(s1 - v1[0]) * (1.0 / z)
    return rank2, e2, count1, e1z


def _peer_kernel(x_ref, g_ref, wq_ref, sk_ref, u_ref, v_ref, o_ref,
                 hn_s, s_s, cp_s, r2_s, e2_s, c1_s, e1_s, a0_s, a1_s, w_s, y_s,
                 *, tm, te, tc, kb, nt):
    j = pl.program_id(1)
    nlc = tm // LANES
    gi = te // PEER_NKEYS
    packed = (PEER_NKEYS // GATE_ROWS, GATE_ROWS, LANES)

    @pl.when(j == 0)
    def _():
        hn = _rmsnorm(x_ref[...], g_ref[...]).astype(BF16)
        hn_s[...] = hn
        for h in range(PEER_HEADS):
            qh = _dot(hn, wq_ref[:, h * PEER_DKEY:(h + 1) * PEER_DKEY]).astype(BF16)
            for p in range(2):
                half = PEER_DKEY // 2
                st = _dot_nt(sk_ref[h, p], qh[:, p * half:(p + 1) * half])
                for ck in range(nlc):
                    s_s[2 * h + p, ck] = st[:, ck * LANES:(ck + 1) * LANES]

        def select(idx, carry):
            h = idx // nlc
            ck = idx % nlc
            rank2, e2, count1, e1z = _peer_select(s_s[2 * h, ck], s_s[2 * h + 1, ck], cp_s)
            flat = (PEER_NKEYS, LANES)
            r2_s[h, ck] = rank2.reshape(flat).astype(GATE_DTYPE).reshape(packed)
            e2_s[h, ck] = e2.reshape(flat).astype(GATE_DTYPE).reshape(packed)
            c1_s[h, ck] = count1.reshape(flat)
            e1_s[h, ck] = e1z.reshape(flat)
            return carry

        lax.fori_loop(0, PEER_HEADS * nlc, select, 0)
        y_s[...] = jnp.zeros_like(y_s)

    i1base = jnp.maximum(j - 1, 0) * gi
    per_kb = kb // PEER_NKEYS

    def step(a_w, a_r):
        for c in range(tm // tc if a_w is not None else 0):
            tok = slice(c * tc, (c + 1) * tc)
            a_w[c] = _dot_nt(u_ref[...], hn_s[tok, :])
        for b in range(te // kb if a_r is not None else 0):
            grp, l0 = divmod(b * per_kb, SUBLANES)
            i1s = pl.ds(pl.multiple_of(i1base + grp * SUBLANES, SUBLANES), SUBLANES)
            for ck in range(nlc):
                c, lc = divmod(ck, tc // LANES)
                ls = slice(lc * LANES, (lc + 1) * LANES)
                cnt = [c1_s[h, ck, i1s, :] for h in range(PEER_HEADS)]
                e1 = [e1_s[h, ck, i1s, :] for h in range(PEER_HEADS)]
                rows = range(l0, l0 + per_kb)
                accs = [jnp.zeros(packed, GATE_DTYPE) for _ in rows]
                for h in range(PEER_HEADS):
                    r2 = r2_s[h, ck]
                    e2 = e2_s[h, ck]
                    for n, l in enumerate(rows):
                        cb = jnp.broadcast_to(cnt[h][l:l + 1], (GATE_ROWS, LANES)).astype(GATE_DTYPE)
                        eb = jnp.broadcast_to(e1[h][l:l + 1], (GATE_ROWS, LANES)).astype(GATE_DTYPE)
                        accs[n] = accs[n] + jnp.where(r2 < cb, e2, 0.0) * eb
                for acc, l in zip(accs, rows):
                    rs = slice((grp * SUBLANES + l) * PEER_NKEYS, (grp * SUBLANES + l + 1) * PEER_NKEYS)
                    a = a_r[c, rs, ls].reshape(packed)
                    gelu = 0.5 * a * (1.0 + lax.erf(a * math.sqrt(0.5)))
                    w = (acc * gelu.astype(GATE_DTYPE)).reshape(PEER_NKEYS, LANES).astype(BF16)
                    w_s[ck * LANES:(ck + 1) * LANES, rs] = w.T
            es = slice(b * kb, (b + 1) * kb)
            y_s[...] += _dot(w_s[:, es], v_ref[es, :])

    inner = (j > 0) & (j < nt)

    @pl.when(j == 0)
    def _():
        step(a0_s, None)

    @pl.when(inner & (j % 2 == 0))
    def _():
        step(a0_s, a1_s)

    @pl.when(inner & (j % 2 == 1))
    def _():
        step(a1_s, a0_s)

    @pl.when(j == nt)
    def _():
        step(None, a1_s if nt % 2 == 0 else a0_s)
        o_ref[...] = x_ref[...] + y_s[...]


def _peer(x2, g, wq, sk, u_tab, v_tab, tm=512, te=1024, tc=256, kb=256):
    n = x2.shape[0]
    nt = u_tab.shape[0] // te
    nlc = tm // LANES
    packed = (PEER_NKEYS // GATE_ROWS, GATE_ROWS, LANES)
    return pl.pallas_call(
        functools.partial(_peer_kernel, tm=tm, te=te, tc=tc, kb=kb, nt=nt),
        grid=(n // tm, nt + 1),
        in_specs=[
            pl.BlockSpec((tm, D_MODEL), lambda i, j: (i, 0)),
            pl.BlockSpec((1, D_MODEL), lambda i, j: (0, 0)),
            pl.BlockSpec(wq.shape, lambda i, j: (0, 0)),
            pl.BlockSpec(sk.shape, lambda i, j: (0, 0, 0, 0)),
            pl.BlockSpec((te, D_MODEL), lambda i, j: (jnp.minimum(j, nt - 1), 0)),
            pl.BlockSpec((te, D_MODEL), lambda i, j: (jnp.maximum(j - 1, 0), 0)),
        ],
        out_specs=pl.BlockSpec((tm, D_MODEL), lambda i, j: (i, 0)),
        out_shape=jax.ShapeDtypeStruct((n, D_MODEL), F32),
        scratch_shapes=[
            pltpu.VMEM((tm, D_MODEL), BF16),
            pltpu.VMEM((2 * PEER_HEADS, nlc, PEER_NKEYS, LANES), F32),
            pltpu.VMEM((PEER_TOPK, LANES), F32),
            pltpu.VMEM((PEER_HEADS, nlc) + packed, GATE_DTYPE),
            pltpu.VMEM((PEER_HEADS, nlc) + packed, GATE_DTYPE),
            pltpu.VMEM((PEER_HEADS, nlc, PEER_NKEYS, LANES), F32),
            pltpu.VMEM((PEER_HEADS, nlc, PEER_NKEYS, LANES), F32),
            pltpu.VMEM((tm // tc, te, tc), F32),
            pltpu.VMEM((tm // tc, te, tc), F32),
            pltpu.VMEM((tm, te), BF16),
            pltpu.VMEM((tm, D_MODEL), F32),
        ],
        compiler_params=_cparams(2),
        name="peer",
    )(x2, g, wq, sk, u_tab, v_tab)


def _rope_tables(pos):
    half = HEAD_DIM // 2
    inv_freq = jnp.power(jnp.float32(ROPE_THETA), -jnp.arange(half, dtype=F32) * (2.0 / HEAD_DIM))
    ang = pos.astype(F32)[:, None] * inv_freq[None, :]
    cos = jnp.concatenate([jnp.cos(ang)] * 4, axis=1)
    sin = jnp.concatenate([jnp.sin(ang)] * 4, axis=1)
    return cos, sin


def kernel(x_prompt, x_sample, state_pool, cache_k, cache_v, norm_mix, norm_ffn, pool_w_in, pool_w_group,
           pool_scale, attn_w_qkv, attn_q_norm, attn_k_norm, attn_sinks, attn_w_o, peer_w_q, peer_subkeys,
           peer_u, peer_v):
    b, s, _ = x_prompt.shape
    db, t, _ = x_sample.shape
    past_len = 16384
    depth = norm_mix.shape[0]
    xp = x_prompt
    xs = x_sample.reshape(db * t, D_MODEL)
    seg = jnp.asarray(np.kron(np.eye(256 // HEAD_DIM), np.ones((HEAD_DIM, HEAD_DIM))), BF16)
    cos_p, sin_p = _rope_tables(jnp.arange(s, dtype=jnp.int32))
    cos_s, sin_s = _rope_tables(past_len + jnp.arange(t, dtype=jnp.int32))
    pool_p, pool_s, kp_l, vp_l, ks_l, vs_l = [], [], [], [], [], []
    for i in range(depth):
        jl = i // 2
        g_mix = norm_mix[i][None, :]
        if i % 2 == 0:
            win = pool_w_in[jl].astype(BF16)
            wg = pool_w_group[jl].astype(BF16)
            sc = pool_scale[jl][None, :]
            xp, st_p = _pool_prompt(xp, g_mix, win, wg, sc)
            prior16 = jnp.pad(state_pool[jl], ((0, 0), (1, 0), (0, 0)))
            xs, st_s = _pool_sample(xs, prior16, g_mix, win, wg, sc, t)
            pool_p.append(st_p[:, 1:])
            pool_s.append(st_s[:, 1:])
        else:
            wqkv = attn_w_qkv[jl].astype(BF16)
            wo = attn_w_o[jl].astype(BF16)
            gain = jnp.concatenate([jnp.tile(attn_q_norm[jl], N_HEADS), jnp.tile(attn_k_norm[jl], N_KV_HEADS)])[None, :]
            sinks = attn_sinks[jl]
            tq = 512
            qp, kp, vp = _qkv(xp.reshape(b * s, D_MODEL), g_mix, wqkv, seg, gain, cos_p, sin_p, tq, s // tq)
            kp3 = kp.reshape(b, s, KV_WIDTH)
            vp3 = vp.reshape(b, s, KV_WIDTH)
            xp = _attn_prompt(xp, qp.reshape(b, s, Q_WIDTH), kp3, vp3, sinks, wo)
            cos_st = jnp.tile(cos_s, (tq // t, 1))
            sin_st = jnp.tile(sin_s, (tq // t, 1))
            qs, kq, vq = _qkv(xs, g_mix, wqkv, seg, gain, cos_st, sin_st, tq, 1)
            ck = cache_k[jl].reshape(db, WINDOW, KV_WIDTH)
            cv = cache_v[jl].reshape(db, WINDOW, KV_WIDTH)
            xs, nk_s, nv_s = _attn_sample(xs, qs, kq, vq, ck, cv, sinks, wo, t)
            kp_l.append(kp3[:, -WINDOW:].reshape(b, WINDOW, N_KV_HEADS, HEAD_DIM))
            vp_l.append(vp3[:, -WINDOW:].reshape(b, WINDOW, N_KV_HEADS, HEAD_DIM))
            ks_l.append(nk_s.reshape(db, WINDOW, N_KV_HEADS, HEAD_DIM))
            vs_l.append(nv_s.reshape(db, WINDOW, N_KV_HEADS, HEAD_DIM))
        g_ffn = norm_ffn[i][None, :]
        wq = peer_w_q[i].astype(BF16)
        sk = peer_subkeys[i].astype(BF16)
        u_tab = peer_u[i].astype(BF16)
        v_tab = peer_v[i].astype(BF16)
        xp = _peer(xp.reshape(b * s, D_MODEL), g_ffn, wq, sk, u_tab, v_tab).reshape(b, s, D_MODEL)
        xs = _peer(xs, g_ffn, wq, sk, u_tab, v_tab)
    return (xp, xs.reshape(db, t, D_MODEL), jnp.stack(pool_p), jnp.stack(pool_s), jnp.stack(kp_l),
            jnp.stack(vp_l), jnp.stack(ks_l), jnp.stack(vs_l))
```

```python
import functools
import math

import jax
import jax.numpy as jnp
import numpy as np
from jax import lax
from jax.experimental import pallas as pl
from jax.experimental.pallas import tpu as pltpu

F32 = jnp.float32
BF16 = jnp.bfloat16

D_MODEL = 1024
LANES = 128
SUBLANES = 8
GATE_DTYPE = jnp.float32
GATE_ROWS = 8
NORM_EPS = 1e-6
NEG_INF = float("-inf")

POOL_WINDOWS = (2, 4, 8, 16)
POOL_GROUP_DIM = D_MODEL // len(POOL_WINDOWS)
POOL_HIST = 16

HEAD_DIM = 64
N_HEADS = 16
N_KV_HEADS = 4
GQA_GROUP = N_HEADS // N_KV_HEADS
WINDOW = 128
PAST_LEN = 16384
ROPE_THETA = 10000.0
ATTN_SCALE = 1.0 / math.sqrt(HEAD_DIM)
Q_WIDTH = N_HEADS * HEAD_DIM
KV_WIDTH = N_KV_HEADS * HEAD_DIM
MASK_NEG = -1e30

PEER_HEADS = 8
PEER_NKEYS = 128
PEER_TOPK = 16
PEER_DKEY = 256

VMEM_LIMIT = 56 * 1024 * 1024


def _cparams(n_axes, **kw):
    return pltpu.CompilerParams(dimension_semantics=("arbitrary",) * n_axes,
                                vmem_limit_bytes=VMEM_LIMIT, **kw)


def _rmsnorm(x, g):
    return x * lax.rsqrt(jnp.mean(x * x, axis=-1, keepdims=True) + NORM_EPS) * g


def _dot(a, b):
    return jnp.dot(a, b, preferred_element_type=F32)


def _dot_nt(a, b):
    return lax.dot_general(a, b, (((1,), (1,)), ((), ())), preferred_element_type=F32)


def _dot_tn(a, b):
    return lax.dot_general(a, b, (((0,), (0,)), ((), ())), preferred_element_type=F32)


def _pool_prompt_kernel(x_ref, g_ref, win_ref, wg_ref, sc_ref, y_ref, st_ref, ext_s, *, ts):
    sb = pl.program_id(1)

    @pl.when(sb == 0)
    def _():
        ext_s[0:POOL_HIST, :] = jnp.zeros((POOL_HIST, D_MODEL), F32)

    x = x_ref[0]
    h = _rmsnorm(x, g_ref[...])
    u = _dot(h.astype(BF16), win_ref[...])
    ext_s[POOL_HIST:POOL_HIST + ts, :] = u
    pos = sb * ts + lax.broadcasted_iota(jnp.int32, (ts, 1), 0)
    outs = []
    for g, w in enumerate(POOL_WINDOWS):
        c0 = g * POOL_GROUP_DIM
        ug = u[:, c0:c0 + POOL_GROUP_DIM]
        win_sum = ug
        for j in range(1, w):
            win_sum = win_sum + ext_s[POOL_HIST - j:POOL_HIST - j + ts, c0:c0 + POOL_GROUP_DIM]
        count = jnp.minimum(pos + 1, w).astype(F32)
        pooled = win_sum / count - ug
        outs.append(_dot(pooled.astype(BF16), wg_ref[g]))
    y = jnp.concatenate(outs, axis=1) * sc_ref[...]
    y_ref[0] = x + y
    st_ref[0] = u[ts - POOL_HIST:ts, :]
    ext_s[0:POOL_HIST, :] = u[ts - POOL_HIST:ts, :]


def _pool_prompt(x, g, win, wg, sc, ts=512):
    b, s, _ = x.shape
    return pl.pallas_call(
        functools.partial(_pool_prompt_kernel, ts=ts),
        grid=(b, s // ts),
        in_specs=[
            pl.BlockSpec((1, ts, D_MODEL), lambda i, j: (i, j, 0)),
            pl.BlockSpec((1, D_MODEL), lambda i, j: (0, 0)),
            pl.BlockSpec((D_MODEL, D_MODEL), lambda i, j: (0, 0)),
            pl.BlockSpec((len(POOL_WINDOWS), POOL_GROUP_DIM, POOL_GROUP_DIM), lambda i, j: (0, 0, 0)),
            pl.BlockSpec((1, D_MODEL), lambda i, j: (0, 0)),
        ],
        out_specs=[
            pl.BlockSpec((1, ts, D_MODEL), lambda i, j: (i, j, 0)),
            pl.BlockSpec((1, POOL_HIST, D_MODEL), lambda i, j: (i, 0, 0)),
        ],
        out_shape=[
            jax.ShapeDtypeStruct((b, s, D_MODEL), F32),
            jax.ShapeDtypeStruct((b, POOL_HIST, D_MODEL), F32),
        ],
        scratch_shapes=[pltpu.VMEM((POOL_HIST + ts, D_MODEL), F32)],
        compiler_params=_cparams(2),
        name="pool_prompt",
    )(x, g, win, wg, sc)


def _pool_sample_kernel(x_ref, pr_ref, g_ref, win_ref, wg_ref, sc_ref, y_ref, st_ref, ext_s, *, bb, t):
    x = x_ref[...]
    h = _rmsnorm(x, g_ref[...])
    u = _dot(h.astype(BF16), win_ref[...])
    ext_s[:, 0:POOL_HIST, :] = pr_ref[...]
    ext_s[:, POOL_HIST:POOL_HIST + t, :] = u.reshape(bb, t, D_MODEL)
    outs = []
    for g, w in enumerate(POOL_WINDOWS):
        c0 = g * POOL_GROUP_DIM
        win_sum = ext_s[:, POOL_HIST:POOL_HIST + t, c0:c0 + POOL_GROUP_DIM]
        for j in range(1, w):
            win_sum = win_sum + ext_s[:, POOL_HIST - j:POOL_HIST - j + t, c0:c0 + POOL_GROUP_DIM]
        pooled = win_sum.reshape(bb * t, POOL_GROUP_DIM) / float(w) - u[:, c0:c0 + POOL_GROUP_DIM]
        outs.append(_dot(pooled.astype(BF16), wg_ref[g]))
    y = jnp.concatenate(outs, axis=1) * sc_ref[...]
    y_ref[...] = x + y
    st_ref[...] = ext_s[:, t:t + POOL_HIST, :]


def _pool_sample(x2, prior16, g, win, wg, sc, t, bb=32):
    n = x2.shape[0]
    nb = n // t
    return pl.pallas_call(
        functools.partial(_pool_sample_kernel, bb=bb, t=t),
        grid=(nb // bb,),
        in_specs=[
            pl.BlockSpec((bb * t, D_MODEL), lambda i: (i, 0)),
            pl.BlockSpec((bb, POOL_HIST, D_MODEL), lambda i: (i, 0, 0)),
            pl.BlockSpec((1, D_MODEL), lambda i: (0, 0)),
            pl.BlockSpec((D_MODEL, D_MODEL), lambda i: (0, 0)),
            pl.BlockSpec((len(POOL_WINDOWS), POOL_GROUP_DIM, POOL_GROUP_DIM), lambda i: (0, 0, 0)),
            pl.BlockSpec((1, D_MODEL), lambda i: (0, 0)),
        ],
        out_specs=[
            pl.BlockSpec((bb * t, D_MODEL), lambda i: (i, 0)),
            pl.BlockSpec((bb, POOL_HIST, D_MODEL), lambda i: (i, 0, 0)),
        ],
        out_shape=[
            jax.ShapeDtypeStruct((n, D_MODEL), F32),
            jax.ShapeDtypeStruct((nb, POOL_HIST, D_MODEL), F32),
        ],
        scratch_shapes=[pltpu.VMEM((bb, POOL_HIST + t, D_MODEL), F32)],
        compiler_params=_cparams(1),
        name="pool_sample",
    )(x2, prior16, g, win, wg, sc)


def _qkv_kernel(x_ref, g_ref, w_ref, seg_ref, gain_ref, cos_ref, sin_ref, q_ref, k_ref, v_ref):
    x = x_ref[...]
    h = _rmsnorm(x, g_ref[...])
    qkv = _dot(h.astype(BF16), w_ref[...])
    nqk = Q_WIDTH + KV_WIDTH
    qk = qkv[:, :nqk]
    sq = qk * qk
    sq_hi = sq.astype(BF16)
    sq_lo = (sq - sq_hi.astype(F32)).astype(BF16)
    seg = seg_ref[...]
    cw = seg.shape[0]
    parts = []
    for c in range(nqk // cw):
        sl = slice(c * cw, (c + 1) * cw)
        parts.append(_dot(sq_hi[:, sl], seg) + _dot(sq_lo[:, sl], seg))
    ssq = jnp.concatenate(parts, axis=1)
    qk = qk * lax.rsqrt(ssq * (1.0 / HEAD_DIM) + NORM_EPS) * gain_ref[...]
    half = HEAD_DIM // 2
    lane = lax.broadcasted_iota(jnp.int32, qk.shape, 1)
    first = (lane % HEAD_DIM) < half
    up = pltpu.roll(qk, nqk - half, 1)
    down = pltpu.roll(qk, half, 1)
    rot = jnp.where(first, -up, down)
    reps = nqk // cos_ref.shape[1]
    cos = jnp.concatenate([cos_ref[...]] * reps, axis=1)
    sin = jnp.concatenate([sin_ref[...]] * reps, axis=1)
    qk = qk * cos + rot * sin
    q_ref[...] = qk[:, :Q_WIDTH]
    k_ref[...] = qk[:, Q_WIDTH:]
    v_ref[...] = qkv[:, nqk:]


def _qkv(x2, g, w, seg, gain, cos, sin, tq, pos_blocks):
    n = x2.shape[0]
    tw = cos.shape[1]
    return pl.pallas_call(
        _qkv_kernel,
        grid=(n // tq,),
        in_specs=[
            pl.BlockSpec((tq, D_MODEL), lambda i: (i, 0)),
            pl.BlockSpec((1, D_MODEL), lambda i: (0, 0)),
            pl.BlockSpec(w.shape, lambda i: (0, 0)),
            pl.BlockSpec(seg.shape, lambda i: (0, 0)),
            pl.BlockSpec((1, Q_WIDTH + KV_WIDTH), lambda i: (0, 0)),
            pl.BlockSpec((tq, tw), lambda i: (i % pos_blocks, 0)),
            pl.BlockSpec((tq, tw), lambda i: (i % pos_blocks, 0)),
        ],
        out_specs=[
            pl.BlockSpec((tq, Q_WIDTH), lambda i: (i, 0)),
            pl.BlockSpec((tq, KV_WIDTH), lambda i: (i, 0)),
            pl.BlockSpec((tq, KV_WIDTH), lambda i: (i, 0)),
        ],
        out_shape=[
            jax.ShapeDtypeStruct((n, Q_WIDTH), F32),
            jax.ShapeDtypeStruct((n, KV_WIDTH), F32),
            jax.ShapeDtypeStruct((n, KV_WIDTH), F32),
        ],
        compiler_params=_cparams(1),
        name="attn_qkv",
    )(x2, g, w, seg, gain, cos, sin)


def _attn_prompt_kernel(x_ref, q_ref, kp_ref, kc_ref, vp_ref, vc_ref, sink_ref, wo_ref, y_ref, *, nsub):
    nb0 = pl.program_id(1) * nsub
    q_all = q_ref[0].astype(BF16)
    k_all = jnp.concatenate([kp_ref[0], kc_ref[0]], axis=0).astype(BF16)
    v_all = jnp.concatenate([vp_ref[0], vc_ref[0]], axis=0).astype(BF16)
    qi = lax.broadcasted_iota(jnp.int32, (WINDOW, 2 * WINDOW), 0)
    kj = lax.broadcasted_iota(jnp.int32, (WINDOW, 2 * WINDOW), 1)
    band = (kj > qi) & (kj <= qi + WINDOW)
    o_blocks = []
    for sub in range(nsub):
        q = q_all[sub * WINDOW:(sub + 1) * WINDOW]
        k2 = k_all[sub * WINDOW:(sub + 2) * WINDOW]
        v2 = v_all[sub * WINDOW:(sub + 2) * WINDOW]
        mask = band & ((nb0 + sub) * WINDOW - WINDOW + kj >= 0)
        outs = []
        for head in range(N_HEADS):
            kh = head // GQA_GROUP
            kk = k2[:, kh * HEAD_DIM:(kh + 1) * HEAD_DIM]
            vv = v2[:, kh * HEAD_DIM:(kh + 1) * HEAD_DIM]
            sk = sink_ref[head]
            s = _dot_nt(q[:, head * HEAD_DIM:(head + 1) * HEAD_DIM], kk) * ATTN_SCALE
            s = jnp.where(mask, s, MASK_NEG)
            m = jnp.maximum(jnp.max(s, axis=-1, keepdims=True), sk)
            p = jnp.exp(s - m)
            p = p / (jnp.sum(p, axis=-1, keepdims=True) + jnp.exp(sk - m))
            outs.append(_dot(p.astype(BF16), vv))
        o_blocks.append(jnp.concatenate(outs, axis=1))
    o = jnp.concatenate(o_blocks, axis=0)
    y_ref[0] = x_ref[0] + _dot(o.astype(BF16), wo_ref[...])


def _attn_prompt(x, q, k, v, sinks, wo, nsub=4):
    b, s, _ = x.shape
    tq = nsub * WINDOW
    cur = lambda i, j: (i, j, 0)
    prev = lambda i, j: (i, jnp.maximum(j * nsub - 1, 0), 0)
    return pl.pallas_call(
        functools.partial(_attn_prompt_kernel, nsub=nsub),
        grid=(b, s // tq),
        in_specs=[
            pl.BlockSpec((1, tq, D_MODEL), cur),
            pl.BlockSpec((1, tq, Q_WIDTH), cur),
            pl.BlockSpec((1, WINDOW, KV_WIDTH), prev),
            pl.BlockSpec((1, tq, KV_WIDTH), cur),
            pl.BlockSpec((1, WINDOW, KV_WIDTH), prev),
            pl.BlockSpec((1, tq, KV_WIDTH), cur),
            pl.BlockSpec(memory_space=pltpu.SMEM),
            pl.BlockSpec((Q_WIDTH, D_MODEL), lambda i, j: (0, 0)),
        ],
        out_specs=pl.BlockSpec((1, tq, D_MODEL), cur),
        out_shape=jax.ShapeDtypeStruct((b, s, D_MODEL), F32),
        compiler_params=_cparams(2),
        name="attn_prompt",
    )(x, q, k, k, v, v, sinks, wo)


def _attn_sample_kernel(x_ref, q_ref, kn_ref, vn_ref, ck_ref, cv_ref, sink_ref, wo_ref,
                        y_ref, nk_ref, nv_ref, *, bb, t):
    q = q_ref[...].reshape(bb, t, Q_WIDTH)
    kn = kn_ref[...].reshape(bb, t, KV_WIDTH)
    vn = vn_ref[...].reshape(bb, t, KV_WIDTH)
    ck = ck_ref[...]
    cv = cv_ref[...]
    nk_ref[:, 0:WINDOW - t, :] = ck[:, t:, :]
    nk_ref[:, WINDOW - t:, :] = kn
    nv_ref[:, 0:WINDOW - t, :] = cv[:, t:, :]
    nv_ref[:, WINDOW - t:, :] = vn
    rows = GQA_GROUP * t
    qi_c = lax.broadcasted_iota(jnp.int32, (bb, rows, WINDOW), 1) % t
    kj_c = lax.broadcasted_iota(jnp.int32, (bb, rows, WINDOW), 2)
    mask_c = kj_c > qi_c
    qi_n = lax.broadcasted_iota(jnp.int32, (bb, rows, t), 1) % t
    kj_n = lax.broadcasted_iota(jnp.int32, (bb, rows, t), 2)
    mask_n = kj_n <= qi_n
    ckb, cvb, knb, vnb = ck.astype(BF16), cv.astype(BF16), kn.astype(BF16), vn.astype(BF16)
    outs = []
    for kh in range(N_KV_HEADS):
        hs = slice(kh * HEAD_DIM, (kh + 1) * HEAD_DIM)
        q4 = jnp.concatenate(
            [q[:, :, (kh * GQA_GROUP + g) * HEAD_DIM:(kh * GQA_GROUP + g + 1) * HEAD_DIM] for g in range(GQA_GROUP)],
            axis=1).astype(BF16)
        sk = jnp.concatenate(
            [jnp.full((1, t, 1), 1.0, F32) * sink_ref[kh * GQA_GROUP + g] for g in range(GQA_GROUP)], axis=1)
        s_c = jnp.einsum("bqd,bkd->bqk", q4, ckb[:, :, hs], preferred_element_type=F32) * ATTN_SCALE
        s_n = jnp.einsum("bqd,bkd->bqk", q4, knb[:, :, hs], preferred_element_type=F32) * ATTN_SCALE
        s_c = jnp.where(mask_c, s_c, MASK_NEG)
        s_n = jnp.where(mask_n, s_n, MASK_NEG)
        m = jnp.maximum(jnp.maximum(jnp.max(s_c, axis=-1, keepdims=True),
                                    jnp.max(s_n, axis=-1, keepdims=True)), sk)
        p_c = jnp.exp(s_c - m)
        p_n = jnp.exp(s_n - m)
        den = jnp.sum(p_c, axis=-1, keepdims=True) + jnp.sum(p_n, axis=-1, keepdims=True) + jnp.exp(sk - m)
        inv = 1.0 / den
        o4 = (jnp.einsum("bqk,bkd->bqd", (p_c * inv).astype(BF16), cvb[:, :, hs], preferred_element_type=F32)
              + jnp.einsum("bqk,bkd->bqd", (p_n * inv).astype(BF16), vnb[:, :, hs], preferred_element_type=F32))
        for g in range(GQA_GROUP):
            outs.append(o4[:, g * t:(g + 1) * t, :])
    o = jnp.concatenate(outs, axis=2).reshape(bb * t, Q_WIDTH)
    y_ref[...] = x_ref[...] + _dot(o.astype(BF16), wo_ref[...])


def _attn_sample(x2, q, kn, vn, ck, cv, sinks, wo, t, bb=16):
    n = x2.shape[0]
    nb = n // t
    row = lambda i: (i, 0)
    cache = lambda i: (i, 0, 0)
    return pl.pallas_call(
        functools.partial(_attn_sample_kernel, bb=bb, t=t),
        grid=(nb // bb,),
        in_specs=[
            pl.BlockSpec((bb * t, D_MODEL), row),
            pl.BlockSpec((bb * t, Q_WIDTH), row),
            pl.BlockSpec((bb * t, KV_WIDTH), row),
            pl.BlockSpec((bb * t, KV_WIDTH), row),
            pl.BlockSpec((bb, WINDOW, KV_WIDTH), cache),
            pl.BlockSpec((bb, WINDOW, KV_WIDTH), cache),
            pl.BlockSpec(memory_space=pltpu.SMEM),
            pl.BlockSpec((Q_WIDTH, D_MODEL), lambda i: (0, 0)),
        ],
        out_specs=[
            pl.BlockSpec((bb * t, D_MODEL), row),
            pl.BlockSpec((bb, WINDOW, KV_WIDTH), cache),
            pl.BlockSpec((bb, WINDOW, KV_WIDTH), cache),
        ],
        out_shape=[
            jax.ShapeDtypeStruct((n, D_MODEL), F32),
            jax.ShapeDtypeStruct((nb, WINDOW, KV_WIDTH), F32),
            jax.ShapeDtypeStruct((nb, WINDOW, KV_WIDTH), F32),
        ],
        compiler_params=_cparams(1),
        name="attn_sample",
    )(x2, q, kn, vn, ck, cv, sinks, wo)


_CAND_LIMITS = tuple(PEER_TOPK // (r2 + 1) for r2 in range(1, SUBLANES))


def _sorting_network(n):
    pairs = []
    p = 1
    while p < n:
        k = p
        while k >= 1:
            for j in range(k % p, n - k, 2 * k):
                for i in range(min(k, n - j - k)):
                    if (i + j) // (2 * p) == (i + j + k) // (2 * p):
                        pairs.append((i + j, i + j + k))
            k //= 2
        p *= 2
    return tuple(pairs)


_SORT_PAIRS = _sorting_network(PEER_TOPK)


def _compare_exchange(x, i, j):
    if x[j] is not None:
        x[i], x[j] = jnp.maximum(x[i], x[j]), jnp.minimum(x[i], x[j])


def _top_sorted(tiles):
    n = len(tiles)
    x = list(tiles) + [None] * (PEER_TOPK - n)
    for i, j in _SORT_PAIRS:
        if j < n:
            _compare_exchange(x, i, j)
    for shift in (4, 2, 1):
        y = [None if v is None else pltpu.roll(v, shift, 0) for v in x]
        m = []
        for k in range(PEER_TOPK):
            a, b = x[k], y[PEER_TOPK - 1 - k]
            m.append(b if a is None else a if b is None else jnp.maximum(a, b))
        d = PEER_TOPK // 2
        while d:
            for k in range(PEER_TOPK):
                if not k & d:
                    _compare_exchange(m, k, k + d)
            d //= 2
        x = m
    return x


def _rank_bits(v, s):
    b8 = v[7] > s
    b4 = jnp.where(b8, v[11], v[3]) > s
    b2 = jnp.where(b8, jnp.where(b4, v[13], v[9]), jnp.where(b4, v[5], v[1])) > s
    piv = [jnp.where(b2, v[4 * i + 2], v[4 * i]) for i in range(4)]
    piv = [jnp.where(b4, piv[2 * i + 1], piv[2 * i]) for i in range(2)]
    b1 = jnp.where(b8, piv[1], piv[0]) > s
    b0 = v[PEER_TOPK - 1] > s
    return b8, b4, b2, b1, b0


def _peer_select(s1, s2, cp_s):
    nt8 = PEER_NKEYS // SUBLANES
    s1 = s1.reshape(nt8, SUBLANES, LANES)
    s2 = s2.reshape(nt8, SUBLANES, LANES)
    v1 = _top_sorted([s1[i] for i in range(nt8)])
    v2 = _top_sorted([s2[i] for i in range(nt8)])
    row8 = lax.broadcasted_iota(jnp.int32, (SUBLANES, LANES), 0)

    def column(v, start):
        col = v[start]
        for s in range(1, SUBLANES):
            col = jnp.where(row8 == s, v[start + s], col)
        return col

    v1_lo, v1_hi, v2_hi = column(v1, 0), column(v1, SUBLANES), column(v2, SUBLANES)
    cand = [v1_lo + v2[0], v1_hi + v2[0]]
    for r2, lim in enumerate(_CAND_LIMITS, start=1):
        g = v1_lo + v2[r2]
        cand.append(jnp.where(row8 < lim, g, NEG_INF) if lim < SUBLANES else g)
    cand.append(v2_hi + v1[0])
    tau = _top_sorted(cand)[PEER_TOPK - 1]
    best = v1[0] + v2[0]
    sel = [c >= tau for c in cand]
    zt = jnp.zeros((SUBLANES, LANES), F32)
    for c, m in zip(cand, sel):
        zt = zt + jnp.where(m, jnp.exp(c - best), 0.0)
    z = jnp.sum(zt, axis=0, keepdims=True)
    self32 = [m.astype(F32) for m in sel]
    lo = self32[0]
    for t in self32[2:-1]:
        lo = lo + t
    tail = jnp.sum(self32[-1], axis=0, keepdims=True)
    cp_s[0:SUBLANES, :] = lo + jnp.where(row8 == 0, tail, 0.0)
    cp_s[SUBLANES:, :] = self32[1]
    bits1 = _rank_bits(v1, s1)
    bits2 = _rank_bits(v2, s2)
    rank2 = jnp.zeros_like(s2)
    for weight, bit in zip((8.0, 4.0, 2.0, 1.0, 1.0), bits2):
        rank2 = rank2 + jnp.where(bit, weight, 0.0)
    level = [cp_s[r:r + 1, :] for r in range(PEER_TOPK)]
    for bit in bits1[3::-1]:
        level = [jnp.where(bit, level[2 * i + 1], level[2 * i]) for i in range(len(level) // 2)]
    count1 = jnp.where(bits1[4], 0.0, level[0])
    e2 = jnp.exp(s2 - v2[0])
    e1z = jnp.exp(s1 - v1[0]) * (1.0 / z)
    return rank2, e2, count1, e1z


def _peer_kernel(x_ref, g_ref, wq_ref, sk_ref, u_ref, v_ref, o_ref,
                 hn_s, s_s, cp_s, r2_s, e2_s, c1_s, e1_s, a0_s, a1_s, w_s, y_s,
                 *, tm, te, tc, kb, nt):
    j = pl.program_id(1)
    nlc = tm // LANES
    gi = te // PEER_NKEYS
    packed = (PEER_NKEYS // GATE_ROWS, GATE_ROWS, LANES)

    @pl.when(j == 0)
    def _():
        hn = _rmsnorm(x_ref[...], g_ref[...]).astype(BF16)
        hn_s[...] = hn
        for h in range(PEER_HEADS):
            qh = _dot(hn, wq_ref[:, h * PEER_DKEY:(h + 1) * PEER_DKEY]).astype(BF16)
            for p in range(2):
                half = PEER_DKEY // 2
                st = _dot_nt(sk_ref[h, p], qh[:, p * half:(p + 1) * half])
                for ck in range(nlc):
                    s_s[2 * h + p, ck] = st[:, ck * LANES:(ck + 1) * LANES]

        def select(idx, carry):
            h = idx // nlc
            ck = idx % nlc
            rank2, e2, count1, e1z = _peer_select(s_s[2 * h, ck], s_s[2 * h + 1, ck], cp_s)
            flat = (PEER_NKEYS, LANES)
            r2_s[h, ck] = rank2.reshape(flat).astype(GATE_DTYPE).reshape(packed)
            e2_s[h, ck] = e2.reshape(flat).astype(GATE_DTYPE).reshape(packed)
            c1_s[h, ck] = count1.reshape(flat)
            e1_s[h, ck] = e1z.reshape(flat)
            return carry

        lax.fori_loop(0, PEER_HEADS * nlc, select, 0)
        y_s[...] = jnp.zeros_like(y_s)

    i1base = jnp.maximum(j - 1, 0) * gi
    per_kb = kb // PEER_NKEYS

    def step(a_w, a_r):
        for c in range(tm // tc if a_w is not None else 0):
            tok = slice(c * tc, (c + 1) * tc)
            a_w[c] = _dot_nt(u_ref[...], hn_s[tok, :])
        for b in range(te // kb if a_r is not None else 0):
            grp, l0 = divmod(b * per_kb, SUBLANES)
            i1s = pl.ds(pl.multiple_of(i1base + grp * SUBLANES, SUBLANES), SUBLANES)
            for ck in range(nlc):
                c, lc = divmod(ck, tc // LANES)
                ls = slice(lc * LANES, (lc + 1) * LANES)
                cnt = [c1_s[h, ck, i1s, :] for h in range(PEER_HEADS)]
                e1 = [e1_s[h, ck, i1s, :] for h in range(PEER_HEADS)]
                rows = range(l0, l0 + per_kb)
                accs = [jnp.zeros(packed, GATE_DTYPE) for _ in rows]
                for h in range(PEER_HEADS):
                    r2 = r2_s[h, ck]
                    e2 = e2_s[h, ck]
                    for n, l in enumerate(rows):
                        cb = jnp.broadcast_to(cnt[h][l:l + 1], (GATE_ROWS, LANES)).astype(GATE_DTYPE)
                        eb = jnp.broadcast_to(e1[h][l:l + 1], (GATE_ROWS, LANES)).astype(GATE_DTYPE)
                        accs[n] = accs[n] + jnp.where(r2 < cb, e2, 0.0) * eb
                for acc, l in zip(accs, rows):
                    rs = slice((grp * SUBLANES + l) * PEER_NKEYS, (grp * SUBLANES + l + 1) * PEER_NKEYS)
                    a = a_r[c, rs, ls].reshape(packed)
                    gelu = 0.5 * a * (1.0 + lax.erf(a * math.sqrt(0.5)))
                    w = (acc * gelu.astype(GATE_DTYPE)).reshape(PEER_NKEYS, LANES).astype(BF16)
                    w_s[ck * LANES:(ck + 1) * LANES, rs] = w.T
            es = slice(b * kb, (b + 1) * kb)
            y_s[...] += _dot(w_s[:, es], v_ref[es, :])

    inner = (j > 0) & (j < nt)

    @pl.when(j == 0)
    def _():
        step(a0_s, None)

    @pl.when(inner & (j % 2 == 0))
    def _():
        step(a0_s, a1_s)

    @pl.when(inner & (j % 2 == 1))
    def _():
        step(a1_s, a0_s)

    @pl.when(j == nt)
    def _():
        step(None, a1_s if nt % 2 == 0 else a0_s)
        o_ref[...] = x_ref[...] + y_s[...]


def _peer(x2, g, wq, sk, u_tab, v_tab, layer, tm=512, te=1024, tc=256, kb=256):
    n = x2.shape[0]
    nt = u_tab.shape[1] // te
    nlc = tm // LANES
    packed = (PEER_NKEYS // GATE_ROWS, GATE_ROWS, LANES)
    return pl.pallas_call(
        functools.partial(_peer_kernel, tm=tm, te=te, tc=tc, kb=kb, nt=nt),
        grid=(n // tm, nt + 1),
        in_specs=[
            pl.BlockSpec((tm, D_MODEL), lambda i, j: (i, 0)),
            pl.BlockSpec((1, D_MODEL), lambda i, j: (0, 0)),
            pl.BlockSpec((None,) + wq.shape[1:], lambda i, j: (layer, 0, 0)),
            pl.BlockSpec((None,) + sk.shape[1:], lambda i, j: (layer, 0, 0, 0, 0)),
            pl.BlockSpec((None, te, D_MODEL), lambda i, j: (layer, jnp.minimum(j, nt - 1), 0)),
            pl.BlockSpec((None, te, D_MODEL), lambda i, j: (layer, jnp.maximum(j - 1, 0), 0)),
        ],
        out_specs=pl.BlockSpec((tm, D_MODEL), lambda i, j: (i, 0)),
        out_shape=jax.ShapeDtypeStruct((n, D_MODEL), F32),
        scratch_shapes=[
            pltpu.VMEM((tm, D_MODEL), BF16),
            pltpu.VMEM((2 * PEER_HEADS, nlc, PEER_NKEYS, LANES), F32),
            pltpu.VMEM((PEER_TOPK, LANES), F32),
            pltpu.VMEM((PEER_HEADS, nlc) + packed, GATE_DTYPE),
            pltpu.VMEM((PEER_HEADS, nlc) + packed, GATE_DTYPE),
            pltpu.VMEM((PEER_HEADS, nlc, PEER_NKEYS, LANES), F32),
            pltpu.VMEM((PEER_HEADS, nlc, PEER_NKEYS, LANES), F32),
            pltpu.VMEM((tm // tc, te, tc), F32),
            pltpu.VMEM((tm // tc, te, tc), F32),
            pltpu.VMEM((tm, te), BF16),
            pltpu.VMEM((tm, D_MODEL), F32),
        ],
        compiler_params=_cparams(2),
        name="peer",
    )(x2, g, wq, sk, u_tab, v_tab)


def _rope_tables(pos):
    half = HEAD_DIM // 2
    inv_freq = jnp.power(jnp.float32(ROPE_THETA), -jnp.arange(half, dtype=F32) * (2.0 / HEAD_DIM))
    ang = pos.astype(F32)[:, None] * inv_freq[None, :]
    cos = jnp.concatenate([jnp.cos(ang)] * 4, axis=1)
    sin = jnp.concatenate([jnp.sin(ang)] * 4, axis=1)
    return cos, sin


def kernel(x_prompt, x_sample, state_pool, cache_k, cache_v, norm_mix, norm_ffn, pool_w_in, pool_w_group,
           pool_scale, attn_w_qkv, attn_q_norm, attn_k_norm, attn_sinks, attn_w_o, peer_w_q, peer_subkeys,
           peer_u, peer_v):
    b, s, _ = x_prompt.shape
    db, t, _ = x_sample.shape
    depth = norm_mix.shape[0]
    wq_all = peer_w_q.astype(BF16)
    sk_all = peer_subkeys.astype(BF16)
    u_all = peer_u.astype(BF16)
    v_all = peer_v.astype(BF16)
    xp = x_prompt
    xs = x_sample.reshape(db * t, D_MODEL)
    seg = jnp.asarray(np.kron(np.eye(256 // HEAD_DIM), np.ones((HEAD_DIM, HEAD_DIM))), BF16)
    cos_p, sin_p = _rope_tables(jnp.arange(s, dtype=jnp.int32))
    cos_s, sin_s = _rope_tables(PAST_LEN + jnp.arange(t, dtype=jnp.int32))
    pool_p, pool_s, kp_l, vp_l, ks_l, vs_l = [], [], [], [], [], []
    for i in range(depth):
        jl = i // 2
        g_mix = norm_mix[i][None, :]
        if i % 2 == 0:
            win = pool_w_in[jl].astype(BF16)
            wg = pool_w_group[jl].astype(BF16)
            sc = pool_scale[jl][None, :]
            xp, st_p = _pool_prompt(xp, g_mix, win, wg, sc)
            prior16 = jnp.pad(state_pool[jl], ((0, 0), (1, 0), (0, 0)))
            xs, st_s = _pool_sample(xs, prior16, g_mix, win, wg, sc, t)
            pool_p.append(st_p[:, 1:])
            pool_s.append(st_s[:, 1:])
        else:
            wqkv = attn_w_qkv[jl].astype(BF16)
            wo = attn_w_o[jl].astype(BF16)
            gain = jnp.concatenate([jnp.tile(attn_q_norm[jl], N_HEADS), jnp.tile(attn_k_norm[jl], N_KV_HEADS)])[None, :]
            sinks = attn_sinks[jl]
            tq = 512
            qp, kp, vp = _qkv(xp.reshape(b * s, D_MODEL), g_mix, wqkv, seg, gain, cos_p, sin_p, tq, s // tq)
            kp3 = kp.reshape(b, s, KV_WIDTH)
            vp3 = vp.reshape(b, s, KV_WIDTH)
            xp = _attn_prompt(xp, qp.reshape(b, s, Q_WIDTH), kp3, vp3, sinks, wo)
            cos_st = jnp.tile(cos_s, (tq // t, 1))
            sin_st = jnp.tile(sin_s, (tq // t, 1))
            qs, kq, vq = _qkv(xs, g_mix, wqkv, seg, gain, cos_st, sin_st, tq, 1)
            ck = cache_k[jl].reshape(db, WINDOW, KV_WIDTH)
            cv = cache_v[jl].reshape(db, WINDOW, KV_WIDTH)
            xs, nk_s, nv_s = _attn_sample(xs, qs, kq, vq, ck, cv, sinks, wo, t)
            kp_l.append(kp3[:, -WINDOW:].reshape(b, WINDOW, N_KV_HEADS, HEAD_DIM))
            vp_l.append(vp3[:, -WINDOW:].reshape(b, WINDOW, N_KV_HEADS, HEAD_DIM))
            ks_l.append(nk_s.reshape(db, WINDOW, N_KV_HEADS, HEAD_DIM))
            vs_l.append(nv_s.reshape(db, WINDOW, N_KV_HEADS, HEAD_DIM))
        g_ffn = norm_ffn[i][None, :]
        xp = _peer(xp.reshape(b * s, D_MODEL), g_ffn, wq_all, sk_all, u_all, v_all, i).reshape(b, s, D_MODEL)
        xs = _peer(xs, g_ffn, wq_all, sk_all, u_all, v_all, i)
    return (xp, xs.reshape(db, t, D_MODEL), jnp.stack(pool_p), jnp.stack(pool_s), jnp.stack(kp_l),
            jnp.stack(vp_l), jnp.stack(ks_l), jnp.stack(vs_l))
```

```python
import functools
import math

import jax
import jax.numpy as jnp
import numpy as np
from jax import lax
from jax.experimental import pallas as pl
from jax.experimental.pallas import tpu as pltpu

F32 = jnp.float32
BF16 = jnp.bfloat16

D_MODEL = 1024
LANES = 128
SUBLANES = 8
GATE_DTYPE = jnp.float32
GATE_ROWS = 8
NORM_EPS = 1e-6
NEG_INF = float("-inf")

POOL_WINDOWS = (2, 4, 8, 16)
POOL_GROUP_DIM = D_MODEL // len(POOL_WINDOWS)
POOL_HIST = 16

HEAD_DIM = 64
N_HEADS = 16
N_KV_HEADS = 4
GQA_GROUP = N_HEADS // N_KV_HEADS
WINDOW = 128
PAST_LEN = 16384
ROPE_THETA = 10000.0
ATTN_SCALE = 1.0 / math.sqrt(HEAD_DIM)
Q_WIDTH = N_HEADS * HEAD_DIM
KV_WIDTH = N_KV_HEADS * HEAD_DIM
MASK_NEG = -1e30

PEER_HEADS = 8
PEER_NKEYS = 128
PEER_TOPK = 16
PEER_DKEY = 256

VMEM_LIMIT = 60 * 1024 * 1024


def _cparams(n_axes, **kw):
    return pltpu.CompilerParams(dimension_semantics=("arbitrary",) * n_axes,
                                vmem_limit_bytes=VMEM_LIMIT, **kw)


def _rmsnorm(x, g):
    return x * lax.rsqrt(jnp.mean(x * x, axis=-1, keepdims=True) + NORM_EPS) * g


def _dot(a, b):
    return jnp.dot(a, b, preferred_element_type=F32)


def _dot_nt(a, b):
    return lax.dot_general(a, b, (((1,), (1,)), ((), ())), preferred_element_type=F32)


def _dot_tn(a, b):
    return lax.dot_general(a, b, (((0,), (0,)), ((), ())), preferred_element_type=F32)


def _pool_prompt_kernel(x_ref, g_ref, win_ref, wg_ref, sc_ref, y_ref, st_ref, ext_s, lvl_a, lvl_b, *, ts):
    sb = pl.program_id(1)
    base = SUBLANES
    cur = SUBLANES + POOL_HIST
    nrow = POOL_HIST + ts

    @pl.when(sb == 0)
    def _():
        zeros = jnp.zeros((cur, D_MODEL), F32)
        ext_s[0:cur, :] = zeros
        lvl_a[0:cur, :] = zeros
        lvl_b[0:cur, :] = zeros

    x = x_ref[0]
    h = _rmsnorm(x, g_ref[...])
    u = _dot(h.astype(BF16), win_ref[...])
    ext_s[cur:cur + ts, :] = u

    def doubled(src, shift, c0):
        return src[base:base + nrow, c0:] + src[base - shift:base - shift + nrow, c0:]

    gd = POOL_GROUP_DIM
    lvl_a[base:base + nrow, :] = doubled(ext_s, 1, 0)
    lvl_b[base:base + nrow, gd:] = doubled(lvl_a, 2, gd)
    lvl_a[base:base + nrow, 2 * gd:] = doubled(lvl_b, 4, 2 * gd)
    sums = [lvl_a[cur:cur + ts, 0:gd], lvl_b[cur:cur + ts, gd:2 * gd], lvl_a[cur:cur + ts, 2 * gd:3 * gd],
            lvl_a[cur:cur + ts, 3 * gd:] + lvl_a[cur - 8:cur - 8 + ts, 3 * gd:]]
    pos = sb * ts + lax.broadcasted_iota(jnp.int32, (ts, 1), 0)
    outs = []
    for g, w in enumerate(POOL_WINDOWS):
        ug = u[:, g * gd:(g + 1) * gd]
        inv_count = 1.0 / jnp.minimum(pos + 1, w).astype(F32)
        pooled = sums[g] * inv_count - ug
        outs.append(_dot(pooled.astype(BF16), wg_ref[g]))
    y = jnp.concatenate(outs, axis=1) * sc_ref[...]
    y_ref[0] = x + y
    st_ref[0] = u[ts - POOL_HIST:ts, :]
    ext_s[base:cur, :] = u[ts - POOL_HIST:ts, :]


def _pool_prompt(x, g, win, wg, sc, ts=512):
    b, s, _ = x.shape
    return pl.pallas_call(
        functools.partial(_pool_prompt_kernel, ts=ts),
        grid=(b, s // ts),
        in_specs=[
            pl.BlockSpec((1, ts, D_MODEL), lambda i, j: (i, j, 0)),
            pl.BlockSpec((1, D_MODEL), lambda i, j: (0, 0)),
            pl.BlockSpec((D_MODEL, D_MODEL), lambda i, j: (0, 0)),
            pl.BlockSpec((len(POOL_WINDOWS), POOL_GROUP_DIM, POOL_GROUP_DIM), lambda i, j: (0, 0, 0)),
            pl.BlockSpec((1, D_MODEL), lambda i, j: (0, 0)),
        ],
        out_specs=[
            pl.BlockSpec((1, ts, D_MODEL), lambda i, j: (i, j, 0)),
            pl.BlockSpec((1, POOL_HIST, D_MODEL), lambda i, j: (i, 0, 0)),
        ],
        out_shape=[
            jax.ShapeDtypeStruct((b, s, D_MODEL), F32),
            jax.ShapeDtypeStruct((b, POOL_HIST, D_MODEL), F32),
        ],
        scratch_shapes=[pltpu.VMEM((SUBLANES + POOL_HIST + ts, D_MODEL), F32)] * 3,
        compiler_params=_cparams(2),
        name="pool_prompt",
    )(x, g, win, wg, sc)


def _pool_sample_kernel(x_ref, pr_ref, g_ref, win_ref, wg_ref, sc_ref, y_ref, st_ref, ext_s, *, bb, t):
    x = x_ref[...]
    h = _rmsnorm(x, g_ref[...])
    u = _dot(h.astype(BF16), win_ref[...])
    ext_s[:, 0:POOL_HIST, :] = pr_ref[...]
    ext_s[:, POOL_HIST:POOL_HIST + t, :] = u.reshape(bb, t, D_MODEL)
    outs = []
    for g, w in enumerate(POOL_WINDOWS):
        c0 = g * POOL_GROUP_DIM
        win_sum = ext_s[:, POOL_HIST:POOL_HIST + t, c0:c0 + POOL_GROUP_DIM]
        for j in range(1, w):
            win_sum = win_sum + ext_s[:, POOL_HIST - j:POOL_HIST - j + t, c0:c0 + POOL_GROUP_DIM]
        pooled = win_sum.reshape(bb * t, POOL_GROUP_DIM) / float(w) - u[:, c0:c0 + POOL_GROUP_DIM]
        outs.append(_dot(pooled.astype(BF16), wg_ref[g]))
    y = jnp.concatenate(outs, axis=1) * sc_ref[...]
    y_ref[...] = x + y
    st_ref[...] = ext_s[:, t:t + POOL_HIST, :]


def _pool_sample(x2, prior16, g, win, wg, sc, t, bb=32):
    n = x2.shape[0]
    nb = n // t
    return pl.pallas_call(
        functools.partial(_pool_sample_kernel, bb=bb, t=t),
        grid=(nb // bb,),
        in_specs=[
            pl.BlockSpec((bb * t, D_MODEL), lambda i: (i, 0)),
            pl.BlockSpec((bb, POOL_HIST, D_MODEL), lambda i: (i, 0, 0)),
            pl.BlockSpec((1, D_MODEL), lambda i: (0, 0)),
            pl.BlockSpec((D_MODEL, D_MODEL), lambda i: (0, 0)),
            pl.BlockSpec((len(POOL_WINDOWS), POOL_GROUP_DIM, POOL_GROUP_DIM), lambda i: (0, 0, 0)),
            pl.BlockSpec((1, D_MODEL), lambda i: (0, 0)),
        ],
        out_specs=[
            pl.BlockSpec((bb * t, D_MODEL), lambda i: (i, 0)),
            pl.BlockSpec((bb, POOL_HIST, D_MODEL), lambda i: (i, 0, 0)),
        ],
        out_shape=[
            jax.ShapeDtypeStruct((n, D_MODEL), F32),
            jax.ShapeDtypeStruct((nb, POOL_HIST, D_MODEL), F32),
        ],
        scratch_shapes=[pltpu.VMEM((bb, POOL_HIST + t, D_MODEL), F32)],
        compiler_params=_cparams(1),
        name="pool_sample",
    )(x2, prior16, g, win, wg, sc)


def _qkv_kernel(x_ref, g_ref, w_ref, seg_ref, gain_ref, cos_ref, sin_ref, q_ref, k_ref, v_ref):
    x = x_ref[...]
    h = _rmsnorm(x, g_ref[...])
    qkv = _dot(h.astype(BF16), w_ref[...])
    nqk = Q_WIDTH + KV_WIDTH
    qk = qkv[:, :nqk]
    sq = qk * qk
    sq_hi = sq.astype(BF16)
    sq_lo = (sq - sq_hi.astype(F32)).astype(BF16)
    seg = seg_ref[...]
    cw = seg.shape[0]
    parts = []
    for c in range(nqk // cw):
        sl = slice(c * cw, (c + 1) * cw)
        parts.append(_dot(sq_hi[:, sl], seg) + _dot(sq_lo[:, sl], seg))
    ssq = jnp.concatenate(parts, axis=1)
    qk = qk * lax.rsqrt(ssq * (1.0 / HEAD_DIM) + NORM_EPS) * gain_ref[...]
    half = HEAD_DIM // 2
    lane = lax.broadcasted_iota(jnp.int32, qk.shape, 1)
    first = (lane % HEAD_DIM) < half
    up = pltpu.roll(qk, nqk - half, 1)
    down = pltpu.roll(qk, half, 1)
    rot = jnp.where(first, -up, down)
    reps = nqk // cos_ref.shape[1]
    cos = jnp.concatenate([cos_ref[...]] * reps, axis=1)
    sin = jnp.concatenate([sin_ref[...]] * reps, axis=1)
    qk = qk * cos + rot * sin
    q_ref[...] = qk[:, :Q_WIDTH]
    k_ref[...] = qk[:, Q_WIDTH:]
    v_ref[...] = qkv[:, nqk:]


def _qkv(x2, g, w, seg, gain, cos, sin, tq, pos_blocks):
    n = x2.shape[0]
    tw = cos.shape[1]
    return pl.pallas_call(
        _qkv_kernel,
        grid=(n // tq,),
        in_specs=[
            pl.BlockSpec((tq, D_MODEL), lambda i: (i, 0)),
            pl.BlockSpec((1, D_MODEL), lambda i: (0, 0)),
            pl.BlockSpec(w.shape, lambda i: (0, 0)),
            pl.BlockSpec(seg.shape, lambda i: (0, 0)),
            pl.BlockSpec((1, Q_WIDTH + KV_WIDTH), lambda i: (0, 0)),
            pl.BlockSpec((tq, tw), lambda i: (i % pos_blocks, 0)),
            pl.BlockSpec((tq, tw), lambda i: (i % pos_blocks, 0)),
        ],
        out_specs=[
            pl.BlockSpec((tq, Q_WIDTH), lambda i: (i, 0)),
            pl.BlockSpec((tq, KV_WIDTH), lambda i: (i, 0)),
            pl.BlockSpec((tq, KV_WIDTH), lambda i: (i, 0)),
        ],
        out_shape=[
            jax.ShapeDtypeStruct((n, Q_WIDTH), F32),
            jax.ShapeDtypeStruct((n, KV_WIDTH), F32),
            jax.ShapeDtypeStruct((n, KV_WIDTH), F32),
        ],
        compiler_params=_cparams(1),
        name="attn_qkv",
    )(x2, g, w, seg, gain, cos, sin)


def _attn_prompt_kernel(x_ref, q_ref, kp_ref, kc_ref, vp_ref, vc_ref, sink_ref, wo_ref, y_ref, *, nsub):
    nb0 = pl.program_id(1) * nsub
    q_all = q_ref[0].astype(BF16)
    k_all = jnp.concatenate([kp_ref[0], kc_ref[0]], axis=0).astype(BF16)
    v_all = jnp.concatenate([vp_ref[0], vc_ref[0]], axis=0).astype(BF16)
    qi = lax.broadcasted_iota(jnp.int32, (WINDOW, 2 * WINDOW), 0)
    kj = lax.broadcasted_iota(jnp.int32, (WINDOW, 2 * WINDOW), 1)
    band = (kj > qi) & (kj <= qi + WINDOW)
    o_blocks = []
    for sub in range(nsub):
        q = q_all[sub * WINDOW:(sub + 1) * WINDOW]
        k2 = k_all[sub * WINDOW:(sub + 2) * WINDOW]
        v2 = v_all[sub * WINDOW:(sub + 2) * WINDOW]
        mask = band & ((nb0 + sub) * WINDOW - WINDOW + kj >= 0)
        outs = []
        for head in range(N_HEADS):
            kh = head // GQA_GROUP
            kk = k2[:, kh * HEAD_DIM:(kh + 1) * HEAD_DIM]
            vv = v2[:, kh * HEAD_DIM:(kh + 1) * HEAD_DIM]
            sk = sink_ref[head]
            s = _dot_nt(q[:, head * HEAD_DIM:(head + 1) * HEAD_DIM], kk) * ATTN_SCALE
            s = jnp.where(mask, s, MASK_NEG)
            m = jnp.maximum(jnp.max(s, axis=-1, keepdims=True), sk)
            p = jnp.exp(s - m)
            p = p / (jnp.sum(p, axis=-1, keepdims=True) + jnp.exp(sk - m))
            outs.append(_dot(p.astype(BF16), vv))
        o_blocks.append(jnp.concatenate(outs, axis=1))
    o = jnp.concatenate(o_blocks, axis=0)
    y_ref[0] = x_ref[0] + _dot(o.astype(BF16), wo_ref[...])


def _attn_prompt(x, q, k, v, sinks, wo, nsub=4):
    b, s, _ = x.shape
    tq = nsub * WINDOW
    cur = lambda i, j: (i, j, 0)
    prev = lambda i, j: (i, jnp.maximum(j * nsub - 1, 0), 0)
    return pl.pallas_call(
        functools.partial(_attn_prompt_kernel, nsub=nsub),
        grid=(b, s // tq),
        in_specs=[
            pl.BlockSpec((1, tq, D_MODEL), cur),
            pl.BlockSpec((1, tq, Q_WIDTH), cur),
            pl.BlockSpec((1, WINDOW, KV_WIDTH), prev),
            pl.BlockSpec((1, tq, KV_WIDTH), cur),
            pl.BlockSpec((1, WINDOW, KV_WIDTH), prev),
            pl.BlockSpec((1, tq, KV_WIDTH), cur),
            pl.BlockSpec(memory_space=pltpu.SMEM),
            pl.BlockSpec((Q_WIDTH, D_MODEL), lambda i, j: (0, 0)),
        ],
        out_specs=pl.BlockSpec((1, tq, D_MODEL), cur),
        out_shape=jax.ShapeDtypeStruct((b, s, D_MODEL), F32),
        compiler_params=_cparams(2),
        name="attn_prompt",
    )(x, q, k, k, v, v, sinks, wo)


def _attn_sample_kernel(x_ref, q_ref, kn_ref, vn_ref, ck_ref, cv_ref, sink_ref, wo_ref,
                        y_ref, nk_ref, nv_ref, *, bb, t):
    q = q_ref[...].reshape(bb, t, Q_WIDTH)
    kn = kn_ref[...].reshape(bb, t, KV_WIDTH)
    vn = vn_ref[...].reshape(bb, t, KV_WIDTH)
    ck = ck_ref[...]
    cv = cv_ref[...]
    nk_ref[:, 0:WINDOW - t, :] = ck[:, t:, :]
    nk_ref[:, WINDOW - t:, :] = kn
    nv_ref[:, 0:WINDOW - t, :] = cv[:, t:, :]
    nv_ref[:, WINDOW - t:, :] = vn
    rows = GQA_GROUP * t
    qi_c = lax.broadcasted_iota(jnp.int32, (bb, rows, WINDOW), 1) % t
    kj_c = lax.broadcasted_iota(jnp.int32, (bb, rows, WINDOW), 2)
    mask_c = kj_c > qi_c
    qi_n = lax.broadcasted_iota(jnp.int32, (bb, rows, t), 1) % t
    kj_n = lax.broadcasted_iota(jnp.int32, (bb, rows, t), 2)
    mask_n = kj_n <= qi_n
    ckb, cvb, knb, vnb = ck.astype(BF16), cv.astype(BF16), kn.astype(BF16), vn.astype(BF16)
    outs = []
    for kh in range(N_KV_HEADS):
        hs = slice(kh * HEAD_DIM, (kh + 1) * HEAD_DIM)
        q4 = jnp.concatenate(
            [q[:, :, (kh * GQA_GROUP + g) * HEAD_DIM:(kh * GQA_GROUP + g + 1) * HEAD_DIM] for g in range(GQA_GROUP)],
            axis=1).astype(BF16)
        sk = jnp.concatenate(
            [jnp.full((1, t, 1), 1.0, F32) * sink_ref[kh * GQA_GROUP + g] for g in range(GQA_GROUP)], axis=1)
        s_c = jnp.einsum("bqd,bkd->bqk", q4, ckb[:, :, hs], preferred_element_type=F32) * ATTN_SCALE
        s_n = jnp.einsum("bqd,bkd->bqk", q4, knb[:, :, hs], preferred_element_type=F32) * ATTN_SCALE
        s_c = jnp.where(mask_c, s_c, MASK_NEG)
        s_n = jnp.where(mask_n, s_n, MASK_NEG)
        m = jnp.maximum(jnp.maximum(jnp.max(s_c, axis=-1, keepdims=True),
                                    jnp.max(s_n, axis=-1, keepdims=True)), sk)
        p_c = jnp.exp(s_c - m)
        p_n = jnp.exp(s_n - m)
        den = jnp.sum(p_c, axis=-1, keepdims=True) + jnp.sum(p_n, axis=-1, keepdims=True) + jnp.exp(sk - m)
        inv = 1.0 / den
        o4 = (jnp.einsum("bqk,bkd->bqd", (p_c * inv).astype(BF16), cvb[:, :, hs], preferred_element_type=F32)
              + jnp.einsum("bqk,bkd->bqd", (p_n * inv).astype(BF16), vnb[:, :, hs], preferred_element_type=F32))
        for g in range(GQA_GROUP):
            outs.append(o4[:, g * t:(g + 1) * t, :])
    o = jnp.concatenate(outs, axis=2).reshape(bb * t, Q_WIDTH)
    y_ref[...] = x_ref[...] + _dot(o.astype(BF16), wo_ref[...])


def _attn_sample(x2, q, kn, vn, ck, cv, sinks, wo, t, bb=16):
    n = x2.shape[0]
    nb = n // t
    row = lambda i: (i, 0)
    cache = lambda i: (i, 0, 0)
    return pl.pallas_call(
        functools.partial(_attn_sample_kernel, bb=bb, t=t),
        grid=(nb // bb,),
        in_specs=[
            pl.BlockSpec((bb * t, D_MODEL), row),
            pl.BlockSpec((bb * t, Q_WIDTH), row),
            pl.BlockSpec((bb * t, KV_WIDTH), row),
            pl.BlockSpec((bb * t, KV_WIDTH), row),
            pl.BlockSpec((bb, WINDOW, KV_WIDTH), cache),
            pl.BlockSpec((bb, WINDOW, KV_WIDTH), cache),
            pl.BlockSpec(memory_space=pltpu.SMEM),
            pl.BlockSpec((Q_WIDTH, D_MODEL), lambda i: (0, 0)),
        ],
        out_specs=[
            pl.BlockSpec((bb * t, D_MODEL), row),
            pl.BlockSpec((bb, WINDOW, KV_WIDTH), cache),
            pl.BlockSpec((bb, WINDOW, KV_WIDTH), cache),
        ],
        out_shape=[
            jax.ShapeDtypeStruct((n, D_MODEL), F32),
            jax.ShapeDtypeStruct((nb, WINDOW, KV_WIDTH), F32),
            jax.ShapeDtypeStruct((nb, WINDOW, KV_WIDTH), F32),
        ],
        compiler_params=_cparams(1),
        name="attn_sample",
    )(x2, q, kn, vn, ck, cv, sinks, wo)


_CAND_LIMITS = tuple(PEER_TOPK // (r2 + 1) for r2 in range(1, SUBLANES))


def _sorting_network(n):
    pairs = []
    p = 1
    while p < n:
        k = p
        while k >= 1:
            for j in range(k % p, n - k, 2 * k):
                for i in range(min(k, n - j - k)):
                    if (i + j) // (2 * p) == (i + j + k) // (2 * p):
                        pairs.append((i + j, i + j + k))
            k //= 2
        p *= 2
    return tuple(pairs)


_SORT_PAIRS = _sorting_network(PEER_TOPK)


def _compare_exchange(x, i, j):
    if x[j] is not None:
        x[i], x[j] = jnp.maximum(x[i], x[j]), jnp.minimum(x[i], x[j])


def _top_sorted(tiles, smallest_only=False):
    n = len(tiles)
    x = list(tiles) + [None] * (PEER_TOPK - n)
    for i, j in _SORT_PAIRS:
        if j < n:
            _compare_exchange(x, i, j)
    for shift in (4, 2, 1):
        y = [None if v is None else pltpu.roll(v, shift, 0) for v in x]
        m = []
        for k in range(PEER_TOPK):
            a, b = x[k], y[PEER_TOPK - 1 - k]
            m.append(b if a is None else a if b is None else jnp.maximum(a, b))
        if smallest_only and shift == 1:
            while len(m) > 1:
                m = [jnp.minimum(m[2 * i], m[2 * i + 1]) for i in range(len(m) // 2)]
            return m[0]
        d = PEER_TOPK // 2
        while d:
            for k in range(PEER_TOPK):
                if not k & d:
                    _compare_exchange(m, k, k + d)
            d //= 2
        x = m
    return x


def _rank_bits(v, s):
    b8 = v[7] > s
    b4 = jnp.where(b8, v[11], v[3]) > s
    b2 = jnp.where(b8, jnp.where(b4, v[13], v[9]), jnp.where(b4, v[5], v[1])) > s
    piv = [jnp.where(b2, v[4 * i + 2], v[4 * i]) for i in range(4)]
    piv = [jnp.where(b4, piv[2 * i + 1], piv[2 * i]) for i in range(2)]
    b1 = jnp.where(b8, piv[1], piv[0]) > s
    b0 = v[PEER_TOPK - 1] > s
    return b8, b4, b2, b1, b0


def _peer_select(s1, s2, cp_s):
    nt8 = PEER_NKEYS // SUBLANES
    s1 = s1.reshape(nt8, SUBLANES, LANES)
    s2 = s2.reshape(nt8, SUBLANES, LANES)
    v1 = _top_sorted([s1[i] for i in range(nt8)])
    v2 = _top_sorted([s2[i] for i in range(nt8)])
    row8 = lax.broadcasted_iota(jnp.int32, (SUBLANES, LANES), 0)

    def column(v, start):
        col = v[start]
        for s in range(1, SUBLANES):
            col = jnp.where(row8 == s, v[start + s], col)
        return col

    v1_lo, v1_hi, v2_hi = column(v1, 0), column(v1, SUBLANES), column(v2, SUBLANES)
    cand = [v1_lo + v2[0], v1_hi + v2[0]]
    for r2, lim in enumerate(_CAND_LIMITS, start=1):
        g = v1_lo + v2[r2]
        cand.append(jnp.where(row8 < lim, g, NEG_INF) if lim < SUBLANES else g)
    cand.append(v2_hi + v1[0])
    tau = _top_sorted(cand, smallest_only=True)
    best = v1[0] + v2[0]
    sel = [c >= tau for c in cand]
    zt = jnp.zeros((SUBLANES, LANES), F32)
    for c, m in zip(cand, sel):
        zt = zt + jnp.where(m, jnp.exp(c - best), 0.0)
    z = jnp.sum(zt, axis=0, keepdims=True)
    self32 = [m.astype(F32) for m in sel]
    lo = self32[0]
    for t in self32[2:-1]:
        lo = lo + t
    tail = jnp.sum(self32[-1], axis=0, keepdims=True)
    cp_s[0:SUBLANES, :] = lo + jnp.where(row8 == 0, tail, 0.0)
    cp_s[SUBLANES:, :] = self32[1]
    bits1 = _rank_bits(v1, s1)
    bits2 = _rank_bits(v2, s2)
    rank2 = jnp.zeros_like(s2)
    for weight, bit in zip((8.0, 4.0, 2.0, 1.0, 1.0), bits2):
        rank2 = rank2 + jnp.where(bit, weight, 0.0)
    level = [cp_s[r:r + 1, :] for r in range(PEER_TOPK)]
    for bit in bits1[3::-1]:
        level = [jnp.where(bit, level[2 * i + 1], level[2 * i]) for i in range(len(level) // 2)]
    count1 = jnp.where(bits1[4], 0.0, level[0])
    e2 = jnp.exp(s2 - v2[0])
    e1z = jnp.exp(s1 - v1[0]) * (1.0 / z)
    return rank2, e2, count1, e1z


def _peer_kernel(x_ref, g_ref, wq_ref, sk_ref, u_ref, v_ref, o_ref,
                 hn_s, q_s, s_s, cp_s, r2_s, e2_s, c1_s, e1_s, a0_s, a1_s, w_s, y_s,
                 *, tm, te, tc, kb, nt):
    j = pl.program_id(1)
    nlc = tm // LANES
    gi = te // PEER_NKEYS
    packed = (PEER_NKEYS // GATE_ROWS, GATE_ROWS, LANES)

    @pl.when(j == 0)
    def _():
        hn = _rmsnorm(x_ref[...], g_ref[...]).astype(BF16)
        hn_s[...] = hn
        q_s[...] = _dot(hn, wq_ref[...]).astype(BF16)
        half = PEER_DKEY // 2
        for h in range(PEER_HEADS):
            for p in range(2):
                q_hp = q_s[:, (2 * h + p) * half:(2 * h + p + 1) * half]
                st = _dot_nt(sk_ref[h, p], q_hp)
                for ck in range(nlc):
                    s_s[2 * h + p, ck] = st[:, ck * LANES:(ck + 1) * LANES]

        def select(idx, carry):
            h = idx // nlc
            ck = idx % nlc
            rank2, e2, count1, e1z = _peer_select(s_s[2 * h, ck], s_s[2 * h + 1, ck], cp_s)
            flat = (PEER_NKEYS, LANES)
            r2_s[h, ck] = rank2.reshape(flat).astype(GATE_DTYPE).reshape(packed)
            e2_s[h, ck] = e2.reshape(flat).astype(GATE_DTYPE).reshape(packed)
            c1_s[h, ck] = count1.reshape(flat)
            e1_s[h, ck] = e1z.reshape(flat)
            return carry

        lax.fori_loop(0, PEER_HEADS * nlc, select, 0)
        y_s[...] = jnp.zeros_like(y_s)

    i1base = jnp.maximum(j - 1, 0) * gi
    per_kb = kb // PEER_NKEYS

    def step(a_w, a_r):
        for c in range(tm // tc if a_w is not None else 0):
            tok = slice(c * tc, (c + 1) * tc)
            a_w[c] = _dot_nt(u_ref[...], hn_s[tok, :])
        for b in range(te // kb if a_r is not None else 0):
            grp, l0 = divmod(b * per_kb, SUBLANES)
            i1s = pl.ds(pl.multiple_of(i1base + grp * SUBLANES, SUBLANES), SUBLANES)
            for ck in range(nlc):
                c, lc = divmod(ck, tc // LANES)
                ls = slice(lc * LANES, (lc + 1) * LANES)
                cnt = [c1_s[h, ck, i1s, :] for h in range(PEER_HEADS)]
                e1 = [e1_s[h, ck, i1s, :] for h in range(PEER_HEADS)]
                rows = range(l0, l0 + per_kb)
                accs = [jnp.zeros(packed, GATE_DTYPE) for _ in rows]
                for h in range(PEER_HEADS):
                    r2 = r2_s[h, ck]
                    e2 = e2_s[h, ck]
                    for n, l in enumerate(rows):
                        cb = jnp.broadcast_to(cnt[h][l:l + 1], (GATE_ROWS, LANES)).astype(GATE_DTYPE)
                        eb = jnp.broadcast_to(e1[h][l:l + 1], (GATE_ROWS, LANES)).astype(GATE_DTYPE)
                        accs[n] = accs[n] + jnp.where(r2 < cb, e2, 0.0) * eb
                for acc, l in zip(accs, rows):
                    rs = slice((grp * SUBLANES + l) * PEER_NKEYS, (grp * SUBLANES + l + 1) * PEER_NKEYS)
                    a = a_r[c, rs, ls].reshape(packed)
                    gelu = 0.5 * a * (1.0 + lax.erf(a * math.sqrt(0.5)))
                    w = (acc * gelu.astype(GATE_DTYPE)).reshape(PEER_NKEYS, LANES).astype(BF16)
                    w_s[ck * LANES:(ck + 1) * LANES, rs] = w.T
            es = slice(b * kb, (b + 1) * kb)
            y_s[...] += _dot(w_s[:, es], v_ref[es, :])

    inner = (j > 0) & (j < nt)

    @pl.when(j == 0)
    def _():
        step(a0_s, None)

    @pl.when(inner & (j % 2 == 0))
    def _():
        step(a0_s, a1_s)

    @pl.when(inner & (j % 2 == 1))
    def _():
        step(a1_s, a0_s)

    @pl.when(j == nt)
    def _():
        step(None, a1_s if nt % 2 == 0 else a0_s)
        o_ref[...] = x_ref[...] + y_s[...]


def _peer(x2, g, wq, sk, u_tab, v_tab, layer, tm=512, te=2048, tc=256, kb=256):
    n = x2.shape[0]
    nt = u_tab.shape[1] // te
    nlc = tm // LANES
    packed = (PEER_NKEYS // GATE_ROWS, GATE_ROWS, LANES)
    return pl.pallas_call(
        functools.partial(_peer_kernel, tm=tm, te=te, tc=tc, kb=kb, nt=nt),
        grid=(n // tm, nt + 1),
        in_specs=[
            pl.BlockSpec((tm, D_MODEL), lambda i, j: (i, 0)),
            pl.BlockSpec((1, D_MODEL), lambda i, j: (0, 0)),
            pl.BlockSpec((None,) + wq.shape[1:], lambda i, j: (layer, 0, 0), pipeline_mode=pl.Buffered(1)),
            pl.BlockSpec((None,) + sk.shape[1:], lambda i, j: (layer, 0, 0, 0, 0), pipeline_mode=pl.Buffered(1)),
            pl.BlockSpec((None, te, D_MODEL), lambda i, j: (layer, jnp.minimum(j, nt - 1), 0)),
            pl.BlockSpec((None, te, D_MODEL), lambda i, j: (layer, jnp.maximum(j - 1, 0), 0)),
        ],
        out_specs=pl.BlockSpec((tm, D_MODEL), lambda i, j: (i, 0)),
        out_shape=jax.ShapeDtypeStruct((n, D_MODEL), F32),
        scratch_shapes=[
            pltpu.VMEM((tm, D_MODEL), BF16),
            pltpu.VMEM((tm, PEER_HEADS * PEER_DKEY), BF16),
            pltpu.VMEM((2 * PEER_HEADS, nlc, PEER_NKEYS, LANES), F32),
            pltpu.VMEM((PEER_TOPK, LANES), F32),
            pltpu.VMEM((PEER_HEADS, nlc) + packed, GATE_DTYPE),
            pltpu.VMEM((PEER_HEADS, nlc) + packed, GATE_DTYPE),
            pltpu.VMEM((PEER_HEADS, nlc, PEER_NKEYS, LANES), F32),
            pltpu.VMEM((PEER_HEADS, nlc, PEER_NKEYS, LANES), F32),
            pltpu.VMEM((tm // tc, te, tc), F32),
            pltpu.VMEM((tm // tc, te, tc), F32),
            pltpu.VMEM((tm, te), BF16),
            pltpu.VMEM((tm, D_MODEL), F32),
        ],
        compiler_params=_cparams(2),
        name="peer",
    )(x2, g, wq, sk, u_tab, v_tab)


def _rope_tables(pos):
    half = HEAD_DIM // 2
    inv_freq = jnp.power(jnp.float32(ROPE_THETA), -jnp.arange(half, dtype=F32) * (2.0 / HEAD_DIM))
    ang = pos.astype(F32)[:, None] * inv_freq[None, :]
    cos = jnp.concatenate([jnp.cos(ang)] * 4, axis=1)
    sin = jnp.concatenate([jnp.sin(ang)] * 4, axis=1)
    return cos, sin


def kernel(x_prompt, x_sample, state_pool, cache_k, cache_v, norm_mix, norm_ffn, pool_w_in, pool_w_group,
           pool_scale, attn_w_qkv, attn_q_norm, attn_k_norm, attn_sinks, attn_w_o, peer_w_q, peer_subkeys,
           peer_u, peer_v):
    b, s, _ = x_prompt.shape
    db, t, _ = x_sample.shape
    depth = norm_mix.shape[0]
    wq_all = peer_w_q.astype(BF16)
    sk_all = peer_subkeys.astype(BF16)
    u_all = peer_u.astype(BF16)
    v_all = peer_v.astype(BF16)
    xp = x_prompt
    xs = x_sample.reshape(db * t, D_MODEL)
    seg = jnp.asarray(np.kron(np.eye(256 // HEAD_DIM), np.ones((HEAD_DIM, HEAD_DIM))), BF16)
    cos_p, sin_p = _rope_tables(jnp.arange(s, dtype=jnp.int32))
    cos_s, sin_s = _rope_tables(PAST_LEN + jnp.arange(t, dtype=jnp.int32))
    pool_p, pool_s, kp_l, vp_l, ks_l, vs_l = [], [], [], [], [], []
    for i in range(depth):
        jl = i // 2
        g_mix = norm_mix[i][None, :]
        if i % 2 == 0:
            win = pool_w_in[jl].astype(BF16)
            wg = pool_w_group[jl].astype(BF16)
            sc = pool_scale[jl][None, :]
            xp, st_p = _pool_prompt(xp, g_mix, win, wg, sc)
            prior16 = jnp.pad(state_pool[jl], ((0, 0), (1, 0), (0, 0)))
            xs, st_s = _pool_sample(xs, prior16, g_mix, win, wg, sc, t)
            pool_p.append(st_p[:, 1:])
            pool_s.append(st_s[:, 1:])
        else:
            wqkv = attn_w_qkv[jl].astype(BF16)
            wo = attn_w_o[jl].astype(BF16)
            gain = jnp.concatenate([jnp.tile(attn_q_norm[jl], N_HEADS), jnp.tile(attn_k_norm[jl], N_KV_HEADS)])[None, :]
            sinks = attn_sinks[jl]
            tq = 512
            qp, kp, vp = _qkv(xp.reshape(b * s, D_MODEL), g_mix, wqkv, seg, gain, cos_p, sin_p, tq, s // tq)
            kp3 = kp.reshape(b, s, KV_WIDTH)
            vp3 = vp.reshape(b, s, KV_WIDTH)
            xp = _attn_prompt(xp, qp.reshape(b, s, Q_WIDTH), kp3, vp3, sinks, wo)
            cos_st = jnp.tile(cos_s, (tq // t, 1))
            sin_st = jnp.tile(sin_s, (tq // t, 1))
            qs, kq, vq = _qkv(xs, g_mix, wqkv, seg, gain, cos_st, sin_st, tq, 1)
            ck = cache_k[jl].reshape(db, WINDOW, KV_WIDTH)
            cv = cache_v[jl].reshape(db, WINDOW, KV_WIDTH)
            xs, nk_s, nv_s = _attn_sample(xs, qs, kq, vq, ck, cv, sinks, wo, t)
            kp_l.append(kp3[:, -WINDOW:].reshape(b, WINDOW, N_KV_HEADS, HEAD_DIM))
            vp_l.append(vp3[:, -WINDOW:].reshape(b, WINDOW, N_KV_HEADS, HEAD_DIM))
            ks_l.append(nk_s.reshape(db, WINDOW, N_KV_HEADS, HEAD_DIM))
            vs_l.append(nv_s.reshape(db, WINDOW, N_KV_HEADS, HEAD_DIM))
        g_ffn = norm_ffn[i][None, :]
        xp = _peer(xp.reshape(b * s, D_MODEL), g_ffn, wq_all, sk_all, u_all, v_all, i).reshape(b, s, D_MODEL)
        xs = _peer(xs, g_ffn, wq_all, sk_all, u_all, v_all, i)
    return (xp, xs.reshape(db, t, D_MODEL), jnp.stack(pool_p), jnp.stack(pool_s), jnp.stack(kp_l),
            jnp.stack(vp_l), jnp.stack(ks_l), jnp.stack(vs_l))
```

```python
import functools
import math

import jax
import jax.numpy as jnp
import numpy as np
from jax import lax
from jax.experimental import pallas as pl
from jax.experimental.pallas import tpu as pltpu

F32 = jnp.float32
BF16 = jnp.bfloat16

D_MODEL = 1024
LANES = 128
SUBLANES = 8
GATE_DTYPE = jnp.float32
GATE_ROWS = 8
NORM_EPS = 1e-6
NEG_INF = float("-inf")

POOL_WINDOWS = (2, 4, 8, 16)
POOL_GROUP_DIM = D_MODEL // len(POOL_WINDOWS)
POOL_HIST = 16

HEAD_DIM = 64
N_HEADS = 16
N_KV_HEADS = 4
GQA_GROUP = N_HEADS // N_KV_HEADS
WINDOW = 128
PAST_LEN = 16384
ROPE_THETA = 10000.0
ATTN_SCALE = 1.0 / math.sqrt(HEAD_DIM)
Q_WIDTH = N_HEADS * HEAD_DIM
KV_WIDTH = N_KV_HEADS * HEAD_DIM
MASK_NEG = -1e30

PEER_HEADS = 8
PEER_NKEYS = 128
PEER_TOPK = 16
PEER_DKEY = 256

VMEM_LIMIT = 60 * 1024 * 1024


def _cparams(n_axes, **kw):
    return pltpu.CompilerParams(dimension_semantics=("arbitrary",) * n_axes,
                                vmem_limit_bytes=VMEM_LIMIT, **kw)


def _rmsnorm(x, g):
    return x * lax.rsqrt(jnp.mean(x * x, axis=-1, keepdims=True) + NORM_EPS) * g


def _dot(a, b):
    return jnp.dot(a, b, preferred_element_type=F32)


def _dot_nt(a, b):
    return lax.dot_general(a, b, (((1,), (1,)), ((), ())), preferred_element_type=F32)


def _pool_prompt_kernel(x_ref, g_ref, win_ref, wg_ref, sc_ref, y_ref, st_ref, ext_s, lvl_a, lvl_b, *, ts):
    sb = pl.program_id(1)
    base = SUBLANES
    cur = SUBLANES + POOL_HIST
    nrow = POOL_HIST + ts

    @pl.when(sb == 0)
    def _():
        zeros = jnp.zeros((cur, D_MODEL), F32)
        ext_s[0:cur, :] = zeros
        lvl_a[0:cur, :] = zeros
        lvl_b[0:cur, :] = zeros

    x = x_ref[0]
    h = _rmsnorm(x, g_ref[...])
    u = _dot(h.astype(BF16), win_ref[...])
    ext_s[cur:cur + ts, :] = u

    def doubled(src, shift, c0):
        return src[base:base + nrow, c0:] + src[base - shift:base - shift + nrow, c0:]

    gd = POOL_GROUP_DIM
    lvl_a[base:base + nrow, :] = doubled(ext_s, 1, 0)
    lvl_b[base:base + nrow, gd:] = doubled(lvl_a, 2, gd)
    lvl_a[base:base + nrow, 2 * gd:] = doubled(lvl_b, 4, 2 * gd)
    sums = [lvl_a[cur:cur + ts, 0:gd], lvl_b[cur:cur + ts, gd:2 * gd], lvl_a[cur:cur + ts, 2 * gd:3 * gd],
            lvl_a[cur:cur + ts, 3 * gd:] + lvl_a[cur - 8:cur - 8 + ts, 3 * gd:]]
    pos = sb * ts + lax.broadcasted_iota(jnp.int32, (ts, 1), 0)
    outs = []
    for g, w in enumerate(POOL_WINDOWS):
        ug = u[:, g * gd:(g + 1) * gd]
        inv_count = 1.0 / jnp.minimum(pos + 1, w).astype(F32)
        pooled = sums[g] * inv_count - ug
        outs.append(_dot(pooled.astype(BF16), wg_ref[g]))
    y = jnp.concatenate(outs, axis=1) * sc_ref[...]
    y_ref[0] = x + y
    st_ref[0] = u[ts - POOL_HIST:ts, :]
    ext_s[base:cur, :] = u[ts - POOL_HIST:ts, :]


def _pool_prompt(x, g, win, wg, sc, ts=512):
    b, s, _ = x.shape
    return pl.pallas_call(
        functools.partial(_pool_prompt_kernel, ts=ts),
        grid=(b, s // ts),
        in_specs=[
            pl.BlockSpec((1, ts, D_MODEL), lambda i, j: (i, j, 0)),
            pl.BlockSpec((1, D_MODEL), lambda i, j: (0, 0)),
            pl.BlockSpec((D_MODEL, D_MODEL), lambda i, j: (0, 0)),
            pl.BlockSpec((len(POOL_WINDOWS), POOL_GROUP_DIM, POOL_GROUP_DIM), lambda i, j: (0, 0, 0)),
            pl.BlockSpec((1, D_MODEL), lambda i, j: (0, 0)),
        ],
        out_specs=[
            pl.BlockSpec((1, ts, D_MODEL), lambda i, j: (i, j, 0)),
            pl.BlockSpec((1, POOL_HIST, D_MODEL), lambda i, j: (i, 0, 0)),
        ],
        out_shape=[
            jax.ShapeDtypeStruct((b, s, D_MODEL), F32),
            jax.ShapeDtypeStruct((b, POOL_HIST, D_MODEL), F32),
        ],
        scratch_shapes=[pltpu.VMEM((SUBLANES + POOL_HIST + ts, D_MODEL), F32)] * 3,
        compiler_params=_cparams(2),
        name="pool_prompt",
    )(x, g, win, wg, sc)


def _pool_sample_kernel(x_ref, pr_ref, g_ref, win_ref, wg_ref, sc_ref, y_ref, st_ref, ext_s, *, bb, t):
    x = x_ref[...]
    h = _rmsnorm(x, g_ref[...])
    u = _dot(h.astype(BF16), win_ref[...])
    ext_s[:, 0:POOL_HIST, :] = pr_ref[...]
    ext_s[:, POOL_HIST:POOL_HIST + t, :] = u.reshape(bb, t, D_MODEL)
    outs = []
    for g, w in enumerate(POOL_WINDOWS):
        c0 = g * POOL_GROUP_DIM
        win_sum = ext_s[:, POOL_HIST:POOL_HIST + t, c0:c0 + POOL_GROUP_DIM]
        for j in range(1, w):
            win_sum = win_sum + ext_s[:, POOL_HIST - j:POOL_HIST - j + t, c0:c0 + POOL_GROUP_DIM]
        pooled = win_sum.reshape(bb * t, POOL_GROUP_DIM) / float(w) - u[:, c0:c0 + POOL_GROUP_DIM]
        outs.append(_dot(pooled.astype(BF16), wg_ref[g]))
    y = jnp.concatenate(outs, axis=1) * sc_ref[...]
    y_ref[...] = x + y
    st_ref[...] = ext_s[:, t:t + POOL_HIST, :]


def _pool_sample(x2, prior16, g, win, wg, sc, t, bb=32):
    n = x2.shape[0]
    nb = n // t
    return pl.pallas_call(
        functools.partial(_pool_sample_kernel, bb=bb, t=t),
        grid=(nb // bb,),
        in_specs=[
            pl.BlockSpec((bb * t, D_MODEL), lambda i: (i, 0)),
            pl.BlockSpec((bb, POOL_HIST, D_MODEL), lambda i: (i, 0, 0)),
            pl.BlockSpec((1, D_MODEL), lambda i: (0, 0)),
            pl.BlockSpec((D_MODEL, D_MODEL), lambda i: (0, 0)),
            pl.BlockSpec((len(POOL_WINDOWS), POOL_GROUP_DIM, POOL_GROUP_DIM), lambda i: (0, 0, 0)),
            pl.BlockSpec((1, D_MODEL), lambda i: (0, 0)),
        ],
        out_specs=[
            pl.BlockSpec((bb * t, D_MODEL), lambda i: (i, 0)),
            pl.BlockSpec((bb, POOL_HIST, D_MODEL), lambda i: (i, 0, 0)),
        ],
        out_shape=[
            jax.ShapeDtypeStruct((n, D_MODEL), F32),
            jax.ShapeDtypeStruct((nb, POOL_HIST, D_MODEL), F32),
        ],
        scratch_shapes=[pltpu.VMEM((bb, POOL_HIST + t, D_MODEL), F32)],
        compiler_params=_cparams(1),
        name="pool_sample",
    )(x2, prior16, g, win, wg, sc)


def _qkv_kernel(x_ref, g_ref, w_ref, seg_ref, gain_ref, cos_ref, sin_ref, q_ref, k_ref, v_ref):
    x = x_ref[...]
    h = _rmsnorm(x, g_ref[...])
    qkv = _dot(h.astype(BF16), w_ref[...])
    nqk = Q_WIDTH + KV_WIDTH
    qk = qkv[:, :nqk]
    sq = qk * qk
    sq_hi = sq.astype(BF16)
    sq_lo = (sq - sq_hi.astype(F32)).astype(BF16)
    seg = seg_ref[...]
    cw = seg.shape[0]
    parts = []
    for c in range(nqk // cw):
        sl = slice(c * cw, (c + 1) * cw)
        parts.append(_dot(sq_hi[:, sl], seg) + _dot(sq_lo[:, sl], seg))
    ssq = jnp.concatenate(parts, axis=1)
    qk = qk * lax.rsqrt(ssq * (1.0 / HEAD_DIM) + NORM_EPS) * gain_ref[...]
    half = HEAD_DIM // 2
    lane = lax.broadcasted_iota(jnp.int32, qk.shape, 1)
    first = (lane % HEAD_DIM) < half
    up = pltpu.roll(qk, nqk - half, 1)
    down = pltpu.roll(qk, half, 1)
    rot = jnp.where(first, -up, down)
    reps = nqk // cos_ref.shape[1]
    cos = jnp.concatenate([cos_ref[...]] * reps, axis=1)
    sin = jnp.concatenate([sin_ref[...]] * reps, axis=1)
    qk = qk * cos + rot * sin
    q_ref[...] = qk[:, :Q_WIDTH]
    k_ref[...] = qk[:, Q_WIDTH:]
    v_ref[...] = qkv[:, nqk:]


def _qkv(x2, g, w, seg, gain, cos, sin, tq, pos_blocks):
    n = x2.shape[0]
    tw = cos.shape[1]
    return pl.pallas_call(
        _qkv_kernel,
        grid=(n // tq,),
        in_specs=[
            pl.BlockSpec((tq, D_MODEL), lambda i: (i, 0)),
            pl.BlockSpec((1, D_MODEL), lambda i: (0, 0)),
            pl.BlockSpec(w.shape, lambda i: (0, 0)),
            pl.BlockSpec(seg.shape, lambda i: (0, 0)),
            pl.BlockSpec((1, Q_WIDTH + KV_WIDTH), lambda i: (0, 0)),
            pl.BlockSpec((tq, tw), lambda i: (i % pos_blocks, 0)),
            pl.BlockSpec((tq, tw), lambda i: (i % pos_blocks, 0)),
        ],
        out_specs=[
            pl.BlockSpec((tq, Q_WIDTH), lambda i: (i, 0)),
            pl.BlockSpec((tq, KV_WIDTH), lambda i: (i, 0)),
            pl.BlockSpec((tq, KV_WIDTH), lambda i: (i, 0)),
        ],
        out_shape=[
            jax.ShapeDtypeStruct((n, Q_WIDTH), F32),
            jax.ShapeDtypeStruct((n, KV_WIDTH), F32),
            jax.ShapeDtypeStruct((n, KV_WIDTH), F32),
        ],
        compiler_params=_cparams(1),
        name="attn_qkv",
    )(x2, g, w, seg, gain, cos, sin)


def _attn_prompt_kernel(x_ref, q_ref, kp_ref, kc_ref, vp_ref, vc_ref, sink_ref, wo_ref, y_ref, *, nsub):
    nb0 = pl.program_id(1) * nsub
    q_all = q_ref[0].astype(BF16)
    k_all = jnp.concatenate([kp_ref[0], kc_ref[0]], axis=0).astype(BF16)
    v_all = jnp.concatenate([vp_ref[0], vc_ref[0]], axis=0).astype(BF16)
    qi = lax.broadcasted_iota(jnp.int32, (WINDOW, 2 * WINDOW), 0)
    kj = lax.broadcasted_iota(jnp.int32, (WINDOW, 2 * WINDOW), 1)
    band = (kj > qi) & (kj <= qi + WINDOW)
    o_blocks = []
    for sub in range(nsub):
        q = q_all[sub * WINDOW:(sub + 1) * WINDOW]
        k2 = k_all[sub * WINDOW:(sub + 2) * WINDOW]
        v2 = v_all[sub * WINDOW:(sub + 2) * WINDOW]
        mask = band & ((nb0 + sub) * WINDOW - WINDOW + kj >= 0)
        outs = []
        for head in range(N_HEADS):
            kh = head // GQA_GROUP
            kk = k2[:, kh * HEAD_DIM:(kh + 1) * HEAD_DIM]
            vv = v2[:, kh * HEAD_DIM:(kh + 1) * HEAD_DIM]
            sk = sink_ref[head]
            s = _dot_nt(q[:, head * HEAD_DIM:(head + 1) * HEAD_DIM], kk) * ATTN_SCALE
            s = jnp.where(mask, s, MASK_NEG)
            m = jnp.maximum(jnp.max(s, axis=-1, keepdims=True), sk)
            p = jnp.exp(s - m)
            p = p / (jnp.sum(p, axis=-1, keepdims=True) + jnp.exp(sk - m))
            outs.append(_dot(p.astype(BF16), vv))
        o_blocks.append(jnp.concatenate(outs, axis=1))
    o = jnp.concatenate(o_blocks, axis=0)
    y_ref[0] = x_ref[0] + _dot(o.astype(BF16), wo_ref[...])


def _attn_prompt(x, q, k, v, sinks, wo, nsub=4):
    b, s, _ = x.shape
    tq = nsub * WINDOW
    cur = lambda i, j: (i, j, 0)
    prev = lambda i, j: (i, jnp.maximum(j * nsub - 1, 0), 0)
    return pl.pallas_call(
        functools.partial(_attn_prompt_kernel, nsub=nsub),
        grid=(b, s // tq),
        in_specs=[
            pl.BlockSpec((1, tq, D_MODEL), cur),
            pl.BlockSpec((1, tq, Q_WIDTH), cur),
            pl.BlockSpec((1, WINDOW, KV_WIDTH), prev),
            pl.BlockSpec((1, tq, KV_WIDTH), cur),
            pl.BlockSpec((1, WINDOW, KV_WIDTH), prev),
            pl.BlockSpec((1, tq, KV_WIDTH), cur),
            pl.BlockSpec(memory_space=pltpu.SMEM),
            pl.BlockSpec((Q_WIDTH, D_MODEL), lambda i, j: (0, 0)),
        ],
        out_specs=pl.BlockSpec((1, tq, D_MODEL), cur),
        out_shape=jax.ShapeDtypeStruct((b, s, D_MODEL), F32),
        compiler_params=_cparams(2),
        name="attn_prompt",
    )(x, q, k, k, v, v, sinks, wo)


def _attn_sample_kernel(x_ref, q_ref, kn_ref, vn_ref, ck_ref, cv_ref, sink_ref, wo_ref,
                        y_ref, nk_ref, nv_ref, *, bb, t):
    q = q_ref[...].reshape(bb, t, Q_WIDTH)
    kn = kn_ref[...].reshape(bb, t, KV_WIDTH)
    vn = vn_ref[...].reshape(bb, t, KV_WIDTH)
    ck = ck_ref[...]
    cv = cv_ref[...]
    nk_ref[:, 0:WINDOW - t, :] = ck[:, t:, :]
    nk_ref[:, WINDOW - t:, :] = kn
    nv_ref[:, 0:WINDOW - t, :] = cv[:, t:, :]
    nv_ref[:, WINDOW - t:, :] = vn
    rows = GQA_GROUP * t
    qi_c = lax.broadcasted_iota(jnp.int32, (bb, rows, WINDOW), 1) % t
    kj_c = lax.broadcasted_iota(jnp.int32, (bb, rows, WINDOW), 2)
    mask_c = kj_c > qi_c
    qi_n = lax.broadcasted_iota(jnp.int32, (bb, rows, t), 1) % t
    kj_n = lax.broadcasted_iota(jnp.int32, (bb, rows, t), 2)
    mask_n = kj_n <= qi_n
    ckb, cvb, knb, vnb = ck.astype(BF16), cv.astype(BF16), kn.astype(BF16), vn.astype(BF16)
    outs = []
    for kh in range(N_KV_HEADS):
        hs = slice(kh * HEAD_DIM, (kh + 1) * HEAD_DIM)
        q4 = jnp.concatenate(
            [q[:, :, (kh * GQA_GROUP + g) * HEAD_DIM:(kh * GQA_GROUP + g + 1) * HEAD_DIM] for g in range(GQA_GROUP)],
            axis=1).astype(BF16)
        sk = jnp.concatenate(
            [jnp.full((1, t, 1), 1.0, F32) * sink_ref[kh * GQA_GROUP + g] for g in range(GQA_GROUP)], axis=1)
        s_c = jnp.einsum("bqd,bkd->bqk", q4, ckb[:, :, hs], preferred_element_type=F32) * ATTN_SCALE
        s_n = jnp.einsum("bqd,bkd->bqk", q4, knb[:, :, hs], preferred_element_type=F32) * ATTN_SCALE
        s_c = jnp.where(mask_c, s_c, MASK_NEG)
        s_n = jnp.where(mask_n, s_n, MASK_NEG)
        m = jnp.maximum(jnp.maximum(jnp.max(s_c, axis=-1, keepdims=True),
                                    jnp.max(s_n, axis=-1, keepdims=True)), sk)
        p_c = jnp.exp(s_c - m)
        p_n = jnp.exp(s_n - m)
        den = jnp.sum(p_c, axis=-1, keepdims=True) + jnp.sum(p_n, axis=-1, keepdims=True) + jnp.exp(sk - m)
        inv = 1.0 / den
        o4 = (jnp.einsum("bqk,bkd->bqd", (p_c * inv).astype(BF16), cvb[:, :, hs], preferred_element_type=F32)
              + jnp.einsum("bqk,bkd->bqd", (p_n * inv).astype(BF16), vnb[:, :, hs], preferred_element_type=F32))
        for g in range(GQA_GROUP):
            outs.append(o4[:, g * t:(g + 1) * t, :])
    o = jnp.concatenate(outs, axis=2).reshape(bb * t, Q_WIDTH)
    y_ref[...] = x_ref[...] + _dot(o.astype(BF16), wo_ref[...])


def _attn_sample(x2, q, kn, vn, ck, cv, sinks, wo, t, bb=16):
    n = x2.shape[0]
    nb = n // t
    row = lambda i: (i, 0)
    cache = lambda i: (i, 0, 0)
    return pl.pallas_call(
        functools.partial(_attn_sample_kernel, bb=bb, t=t),
        grid=(nb // bb,),
        in_specs=[
            pl.BlockSpec((bb * t, D_MODEL), row),
            pl.BlockSpec((bb * t, Q_WIDTH), row),
            pl.BlockSpec((bb * t, KV_WIDTH), row),
            pl.BlockSpec((bb * t, KV_WIDTH), row),
            pl.BlockSpec((bb, WINDOW, KV_WIDTH), cache),
            pl.BlockSpec((bb, WINDOW, KV_WIDTH), cache),
            pl.BlockSpec(memory_space=pltpu.SMEM),
            pl.BlockSpec((Q_WIDTH, D_MODEL), lambda i: (0, 0)),
        ],
        out_specs=[
            pl.BlockSpec((bb * t, D_MODEL), row),
            pl.BlockSpec((bb, WINDOW, KV_WIDTH), cache),
            pl.BlockSpec((bb, WINDOW, KV_WIDTH), cache),
        ],
        out_shape=[
            jax.ShapeDtypeStruct((n, D_MODEL), F32),
            jax.ShapeDtypeStruct((nb, WINDOW, KV_WIDTH), F32),
            jax.ShapeDtypeStruct((nb, WINDOW, KV_WIDTH), F32),
        ],
        compiler_params=_cparams(1),
        name="attn_sample",
    )(x2, q, kn, vn, ck, cv, sinks, wo)


_CAND_LIMITS = tuple(PEER_TOPK // (r2 + 1) for r2 in range(1, SUBLANES))


def _sorting_network(n):
    pairs = []
    p = 1
    while p < n:
        k = p
        while k >= 1:
            for j in range(k % p, n - k, 2 * k):
                for i in range(min(k, n - j - k)):
                    if (i + j) // (2 * p) == (i + j + k) // (2 * p):
                        pairs.append((i + j, i + j + k))
            k //= 2
        p *= 2
    return tuple(pairs)


_SORT_PAIRS = _sorting_network(PEER_TOPK)


def _compare_exchange(x, i, j):
    if x[j] is not None:
        x[i], x[j] = jnp.maximum(x[i], x[j]), jnp.minimum(x[i], x[j])


def _top_sorted(tiles, smallest_only=False):
    n = len(tiles)
    x = list(tiles) + [None] * (PEER_TOPK - n)
    for i, j in _SORT_PAIRS:
        if j < n:
            _compare_exchange(x, i, j)
    for shift in (4, 2, 1):
        y = [None if v is None else pltpu.roll(v, shift, 0) for v in x]
        m = []
        for k in range(PEER_TOPK):
            a, b = x[k], y[PEER_TOPK - 1 - k]
            m.append(b if a is None else a if b is None else jnp.maximum(a, b))
        if smallest_only and shift == 1:
            while len(m) > 1:
                m = [jnp.minimum(m[2 * i], m[2 * i + 1]) for i in range(len(m) // 2)]
            return m[0]
        d = PEER_TOPK // 2
        while d:
            for k in range(PEER_TOPK):
                if not k & d:
                    _compare_exchange(m, k, k + d)
            d //= 2
        x = m
    return x


def _rank_bits(v, s):
    b8 = v[7] > s
    b4 = jnp.where(b8, v[11], v[3]) > s
    b2 = jnp.where(b8, jnp.where(b4, v[13], v[9]), jnp.where(b4, v[5], v[1])) > s
    piv = [jnp.where(b2, v[4 * i + 2], v[4 * i]) for i in range(4)]
    piv = [jnp.where(b4, piv[2 * i + 1], piv[2 * i]) for i in range(2)]
    b1 = jnp.where(b8, piv[1], piv[0]) > s
    b0 = v[PEER_TOPK - 1] > s
    return b8, b4, b2, b1, b0


def _peer_select(s1, s2, cp_s):
    nt8 = PEER_NKEYS // SUBLANES
    s1 = s1.reshape(nt8, SUBLANES, LANES)
    s2 = s2.reshape(nt8, SUBLANES, LANES)
    v1 = _top_sorted([s1[i] for i in range(nt8)])
    v2 = _top_sorted([s2[i] for i in range(nt8)])
    row8 = lax.broadcasted_iota(jnp.int32, (SUBLANES, LANES), 0)

    def column(v, start):
        col = v[start]
        for s in range(1, SUBLANES):
            col = jnp.where(row8 == s, v[start + s], col)
        return col

    v1_lo, v1_hi, v2_hi = column(v1, 0), column(v1, SUBLANES), column(v2, SUBLANES)
    cand = [v1_lo + v2[0], v1_hi + v2[0]]
    for r2, lim in enumerate(_CAND_LIMITS, start=1):
        g = v1_lo + v2[r2]
        cand.append(jnp.where(row8 < lim, g, NEG_INF) if lim < SUBLANES else g)
    cand.append(v2_hi + v1[0])
    tau = _top_sorted(cand, smallest_only=True)
    best = v1[0] + v2[0]
    sel = [c >= tau for c in cand]
    zt = jnp.zeros((SUBLANES, LANES), F32)
    for c, m in zip(cand, sel):
        zt = zt + jnp.where(m, jnp.exp(c - best), 0.0)
    z = jnp.sum(zt, axis=0, keepdims=True)
    self32 = [m.astype(F32) for m in sel]
    lo = self32[0]
    for t in self32[2:-1]:
        lo = lo + t
    tail = jnp.sum(self32[-1], axis=0, keepdims=True)
    cp_s[0:SUBLANES, :] = lo + jnp.where(row8 == 0, tail, 0.0)
    cp_s[SUBLANES:, :] = self32[1]
    bits1 = _rank_bits(v1, s1)
    bits2 = _rank_bits(v2, s2)
    rank2 = jnp.zeros_like(s2)
    for weight, bit in zip((8.0, 4.0, 2.0, 1.0, 1.0), bits2):
        rank2 = rank2 + jnp.where(bit, weight, 0.0)
    level = [cp_s[r:r + 1, :] for r in range(PEER_TOPK)]
    for bit in bits1[3::-1]:
        level = [jnp.where(bit, level[2 * i + 1], level[2 * i]) for i in range(len(level) // 2)]
    count1 = jnp.where(bits1[4], 0.0, level[0])
    e2 = jnp.exp(s2 - v2[0])
    e1h = jnp.exp(s1 - v1[0]) * (0.5 / z)
    return rank2, e2, count1, e1h


def _peer_kernel(x_ref, g_ref, wq_ref, sk_ref, u_ref, v_ref, o_ref,
                 hn_s, q_s, s_s, cp_s, r2_s, e2_s, c1_s, e1_s, a0_s, a1_s, w_s, y_s,
                 *, tm, te, tc, kb, nt):
    j = pl.program_id(1)
    nlc = tm // LANES
    gi = te // PEER_NKEYS
    packed = (PEER_NKEYS // GATE_ROWS, GATE_ROWS, LANES)

    @pl.when(j == 0)
    def _():
        hn = _rmsnorm(x_ref[...], g_ref[...]).astype(BF16)
        hn_s[...] = hn
        q_s[...] = _dot(hn, wq_ref[...]).astype(BF16)
        half = PEER_DKEY // 2
        for h in range(PEER_HEADS):
            for p in range(2):
                q_hp = q_s[:, (2 * h + p) * half:(2 * h + p + 1) * half]
                st = _dot_nt(sk_ref[h, p], q_hp)
                for ck in range(nlc):
                    s_s[2 * h + p, ck] = st[:, ck * LANES:(ck + 1) * LANES]

        def select(idx, carry):
            h = idx // nlc
            ck = idx % nlc
            rank2, e2, count1, e1h = _peer_select(s_s[2 * h, ck], s_s[2 * h + 1, ck], cp_s)
            flat = (PEER_NKEYS, LANES)
            r2_s[h, ck] = rank2.reshape(flat).astype(GATE_DTYPE).reshape(packed)
            e2_s[h, ck] = e2.reshape(flat).astype(GATE_DTYPE).reshape(packed)
            c1_s[h, ck] = count1.reshape(flat)
            e1_s[h, ck] = e1h.reshape(flat)
            return carry

        lax.fori_loop(0, PEER_HEADS * nlc, select, 0)
        y_s[...] = jnp.zeros_like(y_s)

    i1base = jnp.maximum(j - 1, 0) * gi
    per_kb = kb // PEER_NKEYS

    def step(a_w, a_r):
        for c in range(tm // tc if a_w is not None else 0):
            tok = slice(c * tc, (c + 1) * tc)
            a_w[c] = _dot_nt(u_ref[...], hn_s[tok, :])
        for b in range(te // kb if a_r is not None else 0):
            grp, l0 = divmod(b * per_kb, SUBLANES)
            i1s = pl.ds(pl.multiple_of(i1base + grp * SUBLANES, SUBLANES), SUBLANES)
            for ck in range(nlc):
                c, lc = divmod(ck, tc // LANES)
                ls = slice(lc * LANES, (lc + 1) * LANES)
                cnt = [c1_s[h, ck, i1s, :] for h in range(PEER_HEADS)]
                e1 = [e1_s[h, ck, i1s, :] for h in range(PEER_HEADS)]
                rows = range(l0, l0 + per_kb)
                accs = [jnp.zeros(packed, GATE_DTYPE) for _ in rows]
                for h in range(PEER_HEADS):
                    r2 = r2_s[h, ck]
                    e2 = e2_s[h, ck]
                    for n, l in enumerate(rows):
                        cb = jnp.broadcast_to(cnt[h][l:l + 1], (GATE_ROWS, LANES)).astype(GATE_DTYPE)
                        eb = jnp.broadcast_to(e1[h][l:l + 1], (GATE_ROWS, LANES)).astype(GATE_DTYPE)
                        accs[n] = accs[n] + jnp.where(r2 < cb, e2, 0.0) * eb
                for acc, l in zip(accs, rows):
                    rs = slice((grp * SUBLANES + l) * PEER_NKEYS, (grp * SUBLANES + l + 1) * PEER_NKEYS)
                    a = a_r[c, rs, ls].reshape(packed)
                    gelu2 = a * (1.0 + lax.erf(a * math.sqrt(0.5)))
                    w = (acc * gelu2.astype(GATE_DTYPE)).reshape(PEER_NKEYS, LANES).astype(BF16)
                    w_s[ck * LANES:(ck + 1) * LANES, rs] = w.T
            es = slice(b * kb, (b + 1) * kb)
            y_s[...] += _dot(w_s[:, es], v_ref[es, :])

    inner = (j > 0) & (j < nt)

    @pl.when(j == 0)
    def _():
        step(a0_s, None)

    @pl.when(inner & (j % 2 == 0))
    def _():
        step(a0_s, a1_s)

    @pl.when(inner & (j % 2 == 1))
    def _():
        step(a1_s, a0_s)

    @pl.when(j == nt)
    def _():
        step(None, a1_s if nt % 2 == 0 else a0_s)
        o_ref[...] = x_ref[...] + y_s[...]


def _peer(x2, g, wq, sk, u_tab, v_tab, layer, tm=512, te=2048, tc=256, kb=256):
    n = x2.shape[0]
    nt = u_tab.shape[1] // te
    nlc = tm // LANES
    packed = (PEER_NKEYS // GATE_ROWS, GATE_ROWS, LANES)
    return pl.pallas_call(
        functools.partial(_peer_kernel, tm=tm, te=te, tc=tc, kb=kb, nt=nt),
        grid=(n // tm, nt + 1),
        in_specs=[
            pl.BlockSpec((tm, D_MODEL), lambda i, j: (i, 0)),
            pl.BlockSpec((1, D_MODEL), lambda i, j: (0, 0)),
            pl.BlockSpec((None,) + wq.shape[1:], lambda i, j: (layer, 0, 0), pipeline_mode=pl.Buffered(1)),
            pl.BlockSpec((None,) + sk.shape[1:], lambda i, j: (layer, 0, 0, 0, 0), pipeline_mode=pl.Buffered(1)),
            pl.BlockSpec((None, te, D_MODEL), lambda i, j: (layer, jnp.minimum(j, nt - 1), 0)),
            pl.BlockSpec((None, te, D_MODEL), lambda i, j: (layer, jnp.maximum(j - 1, 0), 0)),
        ],
        out_specs=pl.BlockSpec((tm, D_MODEL), lambda i, j: (i, 0)),
        out_shape=jax.ShapeDtypeStruct((n, D_MODEL), F32),
        scratch_shapes=[
            pltpu.VMEM((tm, D_MODEL), BF16),
            pltpu.VMEM((tm, PEER_HEADS * PEER_DKEY), BF16),
            pltpu.VMEM((2 * PEER_HEADS, nlc, PEER_NKEYS, LANES), F32),
            pltpu.VMEM((PEER_TOPK, LANES), F32),
            pltpu.VMEM((PEER_HEADS, nlc) + packed, GATE_DTYPE),
            pltpu.VMEM((PEER_HEADS, nlc) + packed, GATE_DTYPE),
            pltpu.VMEM((PEER_HEADS, nlc, PEER_NKEYS, LANES), F32),
            pltpu.VMEM((PEER_HEADS, nlc, PEER_NKEYS, LANES), F32),
            pltpu.VMEM((tm // tc, te, tc), F32),
            pltpu.VMEM((tm // tc, te, tc), F32),
            pltpu.VMEM((tm, te), BF16),
            pltpu.VMEM((tm, D_MODEL), F32),
        ],
        compiler_params=_cparams(2),
        name="peer",
    )(x2, g, wq, sk, u_tab, v_tab)


def _rope_tables(pos):
    half = HEAD_DIM // 2
    inv_freq = jnp.power(jnp.float32(ROPE_THETA), -jnp.arange(half, dtype=F32) * (2.0 / HEAD_DIM))
    ang = pos.astype(F32)[:, None] * inv_freq[None, :]
    cos = jnp.concatenate([jnp.cos(ang)] * 4, axis=1)
    sin = jnp.concatenate([jnp.sin(ang)] * 4, axis=1)
    return cos, sin


def kernel(x_prompt, x_sample, state_pool, cache_k, cache_v, norm_mix, norm_ffn, pool_w_in, pool_w_group,
           pool_scale, attn_w_qkv, attn_q_norm, attn_k_norm, attn_sinks, attn_w_o, peer_w_q, peer_subkeys,
           peer_u, peer_v):
    b, s, _ = x_prompt.shape
    db, t, _ = x_sample.shape
    depth = norm_mix.shape[0]
    wq_all = peer_w_q.astype(BF16)
    sk_all = peer_subkeys.astype(BF16)
    u_all = peer_u.astype(BF16)
    v_all = peer_v.astype(BF16)
    xp = x_prompt
    xs = x_sample.reshape(db * t, D_MODEL)
    seg = jnp.asarray(np.kron(np.eye(256 // HEAD_DIM), np.ones((HEAD_DIM, HEAD_DIM))), BF16)
    cos_p, sin_p = _rope_tables(jnp.arange(s, dtype=jnp.int32))
    cos_s, sin_s = _rope_tables(PAST_LEN + jnp.arange(t, dtype=jnp.int32))
    pool_p, pool_s, kp_l, vp_l, ks_l, vs_l = [], [], [], [], [], []
    for i in range(depth):
        jl = i // 2
        g_mix = norm_mix[i][None, :]
        if i % 2 == 0:
            win = pool_w_in[jl].astype(BF16)
            wg = pool_w_group[jl].astype(BF16)
            sc = pool_scale[jl][None, :]
            xp, st_p = _pool_prompt(xp, g_mix, win, wg, sc)
            prior16 = jnp.pad(state_pool[jl], ((0, 0), (1, 0), (0, 0)))
            xs, st_s = _pool_sample(xs, prior16, g_mix, win, wg, sc, t)
            pool_p.append(st_p[:, 1:])
            pool_s.append(st_s[:, 1:])
        else:
            wqkv = attn_w_qkv[jl].astype(BF16)
            wo = attn_w_o[jl].astype(BF16)
            gain = jnp.concatenate([jnp.tile(attn_q_norm[jl], N_HEADS), jnp.tile(attn_k_norm[jl], N_KV_HEADS)])[None, :]
            sinks = attn_sinks[jl]
            tq = 512
            qp, kp, vp = _qkv(xp.reshape(b * s, D_MODEL), g_mix, wqkv, seg, gain, cos_p, sin_p, tq, s // tq)
            kp3 = kp.reshape(b, s, KV_WIDTH)
            vp3 = vp.reshape(b, s, KV_WIDTH)
            xp = _attn_prompt(xp, qp.reshape(b, s, Q_WIDTH), kp3, vp3, sinks, wo)
            cos_st = jnp.tile(cos_s, (tq // t, 1))
            sin_st = jnp.tile(sin_s, (tq // t, 1))
            qs, kq, vq = _qkv(xs, g_mix, wqkv, seg, gain, cos_st, sin_st, tq, 1)
            ck = cache_k[jl].reshape(db, WINDOW, KV_WIDTH)
            cv = cache_v[jl].reshape(db, WINDOW, KV_WIDTH)
            xs, nk_s, nv_s = _attn_sample(xs, qs, kq, vq, ck, cv, sinks, wo, t)
            kp_l.append(kp3[:, -WINDOW:].reshape(b, WINDOW, N_KV_HEADS, HEAD_DIM))
            vp_l.append(vp3[:, -WINDOW:].reshape(b, WINDOW, N_KV_HEADS, HEAD_DIM))
            ks_l.append(nk_s.reshape(db, WINDOW, N_KV_HEADS, HEAD_DIM))
            vs_l.append(nv_s.reshape(db, WINDOW, N_KV_HEADS, HEAD_DIM))
        g_ffn = norm_ffn[i][None, :]
        xp = _peer(xp.reshape(b * s, D_MODEL), g_ffn, wq_all, sk_all, u_all, v_all, i).reshape(b, s, D_MODEL)
        xs = _peer(xs, g_ffn, wq_all, sk_all, u_all, v_all, i)
    return (xp, xs.reshape(db, t, D_MODEL), jnp.stack(pool_p), jnp.stack(pool_s), jnp.stack(kp_l),
            jnp.stack(vp_l), jnp.stack(ks_l), jnp.stack(vs_l))
```

```python
import functools
import math

import jax
import jax.numpy as jnp
import numpy as np
from jax import lax
from jax.experimental import pallas as pl
from jax.experimental.pallas import tpu as pltpu

F32 = jnp.float32
BF16 = jnp.bfloat16

D_MODEL = 1024
LANES = 128
SUBLANES = 8
POS_INF = float("inf")
NORM_EPS = 1e-6
NEG_INF = float("-inf")

POOL_WINDOWS = (2, 4, 8, 16)
POOL_GROUP_DIM = D_MODEL // len(POOL_WINDOWS)
POOL_HIST = 16

HEAD_DIM = 64
N_HEADS = 16
N_KV_HEADS = 4
GQA_GROUP = N_HEADS // N_KV_HEADS
WINDOW = 128
PAST_LEN = 16384
ROPE_THETA = 10000.0
ATTN_SCALE = 1.0 / math.sqrt(HEAD_DIM)
Q_WIDTH = N_HEADS * HEAD_DIM
KV_WIDTH = N_KV_HEADS * HEAD_DIM
MASK_NEG = -1e30

PEER_HEADS = 8
PEER_NKEYS = 128
PEER_TOPK = 16
PEER_DKEY = 256

VMEM_LIMIT = 60 * 1024 * 1024


def _cparams(n_axes, **kw):
    return pltpu.CompilerParams(dimension_semantics=("arbitrary",) * n_axes,
                                vmem_limit_bytes=VMEM_LIMIT, **kw)


def _rmsnorm(x, g):
    return x * lax.rsqrt(jnp.mean(x * x, axis=-1, keepdims=True) + NORM_EPS) * g


def _dot(a, b):
    return jnp.dot(a, b, preferred_element_type=F32)


def _dot_nt(a, b):
    return lax.dot_general(a, b, (((1,), (1,)), ((), ())), preferred_element_type=F32)


def _pool_prompt_kernel(x_ref, g_ref, win_ref, wg_ref, sc_ref, y_ref, st_ref, ext_s, lvl_a, lvl_b, *, ts):
    sb = pl.program_id(1)
    base = SUBLANES
    cur = SUBLANES + POOL_HIST
    nrow = POOL_HIST + ts

    @pl.when(sb == 0)
    def _():
        zeros = jnp.zeros((cur, D_MODEL), F32)
        ext_s[0:cur, :] = zeros
        lvl_a[0:cur, :] = zeros
        lvl_b[0:cur, :] = zeros

    x = x_ref[0]
    h = _rmsnorm(x, g_ref[...])
    u = _dot(h.astype(BF16), win_ref[...])
    ext_s[cur:cur + ts, :] = u

    def doubled(src, shift, c0):
        return src[base:base + nrow, c0:] + src[base - shift:base - shift + nrow, c0:]

    gd = POOL_GROUP_DIM
    lvl_a[base:base + nrow, :] = doubled(ext_s, 1, 0)
    lvl_b[base:base + nrow, gd:] = doubled(lvl_a, 2, gd)
    lvl_a[base:base + nrow, 2 * gd:] = doubled(lvl_b, 4, 2 * gd)
    sums = [lvl_a[cur:cur + ts, 0:gd], lvl_b[cur:cur + ts, gd:2 * gd], lvl_a[cur:cur + ts, 2 * gd:3 * gd],
            lvl_a[cur:cur + ts, 3 * gd:] + lvl_a[cur - 8:cur - 8 + ts, 3 * gd:]]
    pos = sb * ts + lax.broadcasted_iota(jnp.int32, (ts, 1), 0)
    outs = []
    for g, w in enumerate(POOL_WINDOWS):
        ug = u[:, g * gd:(g + 1) * gd]
        inv_count = 1.0 / jnp.minimum(pos + 1, w).astype(F32)
        pooled = sums[g] * inv_count - ug
        outs.append(_dot(pooled.astype(BF16), wg_ref[g]))
    y = jnp.concatenate(outs, axis=1) * sc_ref[...]
    y_ref[0] = x + y
    st_ref[0] = u[ts - POOL_HIST:ts, :]
    ext_s[base:cur, :] = u[ts - POOL_HIST:ts, :]


def _pool_prompt(x, g, win, wg, sc, ts=512):
    b, s, _ = x.shape
    return pl.pallas_call(
        functools.partial(_pool_prompt_kernel, ts=ts),
        grid=(b, s // ts),
        in_specs=[
            pl.BlockSpec((1, ts, D_MODEL), lambda i, j: (i, j, 0)),
            pl.BlockSpec((1, D_MODEL), lambda i, j: (0, 0)),
            pl.BlockSpec((D_MODEL, D_MODEL), lambda i, j: (0, 0)),
            pl.BlockSpec((len(POOL_WINDOWS), POOL_GROUP_DIM, POOL_GROUP_DIM), lambda i, j: (0, 0, 0)),
            pl.BlockSpec((1, D_MODEL), lambda i, j: (0, 0)),
        ],
        out_specs=[
            pl.BlockSpec((1, ts, D_MODEL), lambda i, j: (i, j, 0)),
            pl.BlockSpec((1, POOL_HIST, D_MODEL), lambda i, j: (i, 0, 0)),
        ],
        out_shape=[
            jax.ShapeDtypeStruct((b, s, D_MODEL), F32),
            jax.ShapeDtypeStruct((b, POOL_HIST, D_MODEL), F32),
        ],
        scratch_shapes=[pltpu.VMEM((SUBLANES + POOL_HIST + ts, D_MODEL), F32)] * 3,
        compiler_params=_cparams(2),
        name="pool_prompt",
    )(x, g, win, wg, sc)


def _pool_sample_kernel(x_ref, pr_ref, g_ref, win_ref, wg_ref, sc_ref, y_ref, st_ref, ext_s, *, bb, t):
    x = x_ref[...]
    h = _rmsnorm(x, g_ref[...])
    u = _dot(h.astype(BF16), win_ref[...])
    ext_s[:, 0:POOL_HIST, :] = pr_ref[...]
    ext_s[:, POOL_HIST:POOL_HIST + t, :] = u.reshape(bb, t, D_MODEL)
    outs = []
    for g, w in enumerate(POOL_WINDOWS):
        c0 = g * POOL_GROUP_DIM
        win_sum = ext_s[:, POOL_HIST:POOL_HIST + t, c0:c0 + POOL_GROUP_DIM]
        for j in range(1, w):
            win_sum = win_sum + ext_s[:, POOL_HIST - j:POOL_HIST - j + t, c0:c0 + POOL_GROUP_DIM]
        pooled = win_sum.reshape(bb * t, POOL_GROUP_DIM) / float(w) - u[:, c0:c0 + POOL_GROUP_DIM]
        outs.append(_dot(pooled.astype(BF16), wg_ref[g]))
    y = jnp.concatenate(outs, axis=1) * sc_ref[...]
    y_ref[...] = x + y
    st_ref[...] = ext_s[:, t:t + POOL_HIST, :]


def _pool_sample(x2, prior16, g, win, wg, sc, t, bb=32):
    n = x2.shape[0]
    nb = n // t
    return pl.pallas_call(
        functools.partial(_pool_sample_kernel, bb=bb, t=t),
        grid=(nb // bb,),
        in_specs=[
            pl.BlockSpec((bb * t, D_MODEL), lambda i: (i, 0)),
            pl.BlockSpec((bb, POOL_HIST, D_MODEL), lambda i: (i, 0, 0)),
            pl.BlockSpec((1, D_MODEL), lambda i: (0, 0)),
            pl.BlockSpec((D_MODEL, D_MODEL), lambda i: (0, 0)),
            pl.BlockSpec((len(POOL_WINDOWS), POOL_GROUP_DIM, POOL_GROUP_DIM), lambda i: (0, 0, 0)),
            pl.BlockSpec((1, D_MODEL), lambda i: (0, 0)),
        ],
        out_specs=[
            pl.BlockSpec((bb * t, D_MODEL), lambda i: (i, 0)),
            pl.BlockSpec((bb, POOL_HIST, D_MODEL), lambda i: (i, 0, 0)),
        ],
        out_shape=[
            jax.ShapeDtypeStruct((n, D_MODEL), F32),
            jax.ShapeDtypeStruct((nb, POOL_HIST, D_MODEL), F32),
        ],
        scratch_shapes=[pltpu.VMEM((bb, POOL_HIST + t, D_MODEL), F32)],
        compiler_params=_cparams(1),
        name="pool_sample",
    )(x2, prior16, g, win, wg, sc)


def _qkv_kernel(x_ref, g_ref, w_ref, seg_ref, gain_ref, cos_ref, sin_ref, q_ref, k_ref, v_ref):
    x = x_ref[...]
    h = _rmsnorm(x, g_ref[...])
    qkv = _dot(h.astype(BF16), w_ref[...])
    nqk = Q_WIDTH + KV_WIDTH
    qk = qkv[:, :nqk]
    sq = qk * qk
    sq_hi = sq.astype(BF16)
    sq_lo = (sq - sq_hi.astype(F32)).astype(BF16)
    seg = seg_ref[...]
    cw = seg.shape[0]
    parts = []
    for c in range(nqk // cw):
        sl = slice(c * cw, (c + 1) * cw)
        parts.append(_dot(sq_hi[:, sl], seg) + _dot(sq_lo[:, sl], seg))
    ssq = jnp.concatenate(parts, axis=1)
    qk = qk * lax.rsqrt(ssq * (1.0 / HEAD_DIM) + NORM_EPS) * gain_ref[...]
    half = HEAD_DIM // 2
    lane = lax.broadcasted_iota(jnp.int32, qk.shape, 1)
    first = (lane % HEAD_DIM) < half
    up = pltpu.roll(qk, nqk - half, 1)
    down = pltpu.roll(qk, half, 1)
    rot = jnp.where(first, -up, down)
    reps = nqk // cos_ref.shape[1]
    cos = jnp.concatenate([cos_ref[...]] * reps, axis=1)
    sin = jnp.concatenate([sin_ref[...]] * reps, axis=1)
    qk = qk * cos + rot * sin
    q_ref[...] = qk[:, :Q_WIDTH]
    k_ref[...] = qk[:, Q_WIDTH:]
    v_ref[...] = qkv[:, nqk:]


def _qkv(x2, g, w, seg, gain, cos, sin, tq, pos_blocks):
    n = x2.shape[0]
    tw = cos.shape[1]
    return pl.pallas_call(
        _qkv_kernel,
        grid=(n // tq,),
        in_specs=[
            pl.BlockSpec((tq, D_MODEL), lambda i: (i, 0)),
            pl.BlockSpec((1, D_MODEL), lambda i: (0, 0)),
            pl.BlockSpec(w.shape, lambda i: (0, 0)),
            pl.BlockSpec(seg.shape, lambda i: (0, 0)),
            pl.BlockSpec((1, Q_WIDTH + KV_WIDTH), lambda i: (0, 0)),
            pl.BlockSpec((tq, tw), lambda i: (i % pos_blocks, 0)),
            pl.BlockSpec((tq, tw), lambda i: (i % pos_blocks, 0)),
        ],
        out_specs=[
            pl.BlockSpec((tq, Q_WIDTH), lambda i: (i, 0)),
            pl.BlockSpec((tq, KV_WIDTH), lambda i: (i, 0)),
            pl.BlockSpec((tq, KV_WIDTH), lambda i: (i, 0)),
        ],
        out_shape=[
            jax.ShapeDtypeStruct((n, Q_WIDTH), F32),
            jax.ShapeDtypeStruct((n, KV_WIDTH), F32),
            jax.ShapeDtypeStruct((n, KV_WIDTH), F32),
        ],
        compiler_params=_cparams(1),
        name="attn_qkv",
    )(x2, g, w, seg, gain, cos, sin)


def _attn_prompt_kernel(x_ref, q_ref, kp_ref, kc_ref, vp_ref, vc_ref, sink_ref, wo_ref, y_ref, *, nsub):
    nb0 = pl.program_id(1) * nsub
    q_all = q_ref[0].astype(BF16)
    k_all = jnp.concatenate([kp_ref[0], kc_ref[0]], axis=0).astype(BF16)
    v_all = jnp.concatenate([vp_ref[0], vc_ref[0]], axis=0).astype(BF16)
    qi = lax.broadcasted_iota(jnp.int32, (WINDOW, 2 * WINDOW), 0)
    kj = lax.broadcasted_iota(jnp.int32, (WINDOW, 2 * WINDOW), 1)
    band = (kj > qi) & (kj <= qi + WINDOW)
    o_blocks = []
    for sub in range(nsub):
        q = q_all[sub * WINDOW:(sub + 1) * WINDOW]
        k2 = k_all[sub * WINDOW:(sub + 2) * WINDOW]
        v2 = v_all[sub * WINDOW:(sub + 2) * WINDOW]
        mask = band & ((nb0 + sub) * WINDOW - WINDOW + kj >= 0)
        outs = []
        for head in range(N_HEADS):
            kh = head // GQA_GROUP
            kk = k2[:, kh * HEAD_DIM:(kh + 1) * HEAD_DIM]
            vv = v2[:, kh * HEAD_DIM:(kh + 1) * HEAD_DIM]
            sk = sink_ref[head]
            s = _dot_nt(q[:, head * HEAD_DIM:(head + 1) * HEAD_DIM], kk) * ATTN_SCALE
            s = jnp.where(mask, s, MASK_NEG)
            m = jnp.maximum(jnp.max(s, axis=-1, keepdims=True), sk)
            p = jnp.exp(s - m)
            p = p / (jnp.sum(p, axis=-1, keepdims=True) + jnp.exp(sk - m))
            outs.append(_dot(p.astype(BF16), vv))
        o_blocks.append(jnp.concatenate(outs, axis=1))
    o = jnp.concatenate(o_blocks, axis=0)
    y_ref[0] = x_ref[0] + _dot(o.astype(BF16), wo_ref[...])


def _attn_prompt(x, q, k, v, sinks, wo, nsub=4):
    b, s, _ = x.shape
    tq = nsub * WINDOW
    cur = lambda i, j: (i, j, 0)
    prev = lambda i, j: (i, jnp.maximum(j * nsub - 1, 0), 0)
    return pl.pallas_call(
        functools.partial(_attn_prompt_kernel, nsub=nsub),
        grid=(b, s // tq),
        in_specs=[
            pl.BlockSpec((1, tq, D_MODEL), cur),
            pl.BlockSpec((1, tq, Q_WIDTH), cur),
            pl.BlockSpec((1, WINDOW, KV_WIDTH), prev),
            pl.BlockSpec((1, tq, KV_WIDTH), cur),
            pl.BlockSpec((1, WINDOW, KV_WIDTH), prev),
            pl.BlockSpec((1, tq, KV_WIDTH), cur),
            pl.BlockSpec(memory_space=pltpu.SMEM),
            pl.BlockSpec((Q_WIDTH, D_MODEL), lambda i, j: (0, 0)),
        ],
        out_specs=pl.BlockSpec((1, tq, D_MODEL), cur),
        out_shape=jax.ShapeDtypeStruct((b, s, D_MODEL), F32),
        compiler_params=_cparams(2),
        name="attn_prompt",
    )(x, q, k, k, v, v, sinks, wo)


def _attn_sample_kernel(x_ref, q_ref, kn_ref, vn_ref, ck_ref, cv_ref, sink_ref, wo_ref,
                        y_ref, nk_ref, nv_ref, *, bb, t):
    q = q_ref[...].reshape(bb, t, Q_WIDTH)
    kn = kn_ref[...].reshape(bb, t, KV_WIDTH)
    vn = vn_ref[...].reshape(bb, t, KV_WIDTH)
    ck = ck_ref[...]
    cv = cv_ref[...]
    nk_ref[:, 0:WINDOW - t, :] = ck[:, t:, :]
    nk_ref[:, WINDOW - t:, :] = kn
    nv_ref[:, 0:WINDOW - t, :] = cv[:, t:, :]
    nv_ref[:, WINDOW - t:, :] = vn
    rows = GQA_GROUP * t
    qi_c = lax.broadcasted_iota(jnp.int32, (bb, rows, WINDOW), 1) % t
    kj_c = lax.broadcasted_iota(jnp.int32, (bb, rows, WINDOW), 2)
    mask_c = kj_c > qi_c
    qi_n = lax.broadcasted_iota(jnp.int32, (bb, rows, t), 1) % t
    kj_n = lax.broadcasted_iota(jnp.int32, (bb, rows, t), 2)
    mask_n = kj_n <= qi_n
    ckb, cvb, knb, vnb = ck.astype(BF16), cv.astype(BF16), kn.astype(BF16), vn.astype(BF16)
    outs = []
    for kh in range(N_KV_HEADS):
        hs = slice(kh * HEAD_DIM, (kh + 1) * HEAD_DIM)
        q4 = jnp.concatenate(
            [q[:, :, (kh * GQA_GROUP + g) * HEAD_DIM:(kh * GQA_GROUP + g + 1) * HEAD_DIM] for g in range(GQA_GROUP)],
            axis=1).astype(BF16)
        sk = jnp.concatenate(
            [jnp.full((1, t, 1), 1.0, F32) * sink_ref[kh * GQA_GROUP + g] for g in range(GQA_GROUP)], axis=1)
        s_c = jnp.einsum("bqd,bkd->bqk", q4, ckb[:, :, hs], preferred_element_type=F32) * ATTN_SCALE
        s_n = jnp.einsum("bqd,bkd->bqk", q4, knb[:, :, hs], preferred_element_type=F32) * ATTN_SCALE
        s_c = jnp.where(mask_c, s_c, MASK_NEG)
        s_n = jnp.where(mask_n, s_n, MASK_NEG)
        m = jnp.maximum(jnp.maximum(jnp.max(s_c, axis=-1, keepdims=True),
                                    jnp.max(s_n, axis=-1, keepdims=True)), sk)
        p_c = jnp.exp(s_c - m)
        p_n = jnp.exp(s_n - m)
        den = jnp.sum(p_c, axis=-1, keepdims=True) + jnp.sum(p_n, axis=-1, keepdims=True) + jnp.exp(sk - m)
        inv = 1.0 / den
        o4 = (jnp.einsum("bqk,bkd->bqd", (p_c * inv).astype(BF16), cvb[:, :, hs], preferred_element_type=F32)
              + jnp.einsum("bqk,bkd->bqd", (p_n * inv).astype(BF16), vnb[:, :, hs], preferred_element_type=F32))
        for g in range(GQA_GROUP):
            outs.append(o4[:, g * t:(g + 1) * t, :])
    o = jnp.concatenate(outs, axis=2).reshape(bb * t, Q_WIDTH)
    y_ref[...] = x_ref[...] + _dot(o.astype(BF16), wo_ref[...])


def _attn_sample(x2, q, kn, vn, ck, cv, sinks, wo, t, bb=16):
    n = x2.shape[0]
    nb = n // t
    row = lambda i: (i, 0)
    cache = lambda i: (i, 0, 0)
    return pl.pallas_call(
        functools.partial(_attn_sample_kernel, bb=bb, t=t),
        grid=(nb // bb,),
        in_specs=[
            pl.BlockSpec((bb * t, D_MODEL), row),
            pl.BlockSpec((bb * t, Q_WIDTH), row),
            pl.BlockSpec((bb * t, KV_WIDTH), row),
            pl.BlockSpec((bb * t, KV_WIDTH), row),
            pl.BlockSpec((bb, WINDOW, KV_WIDTH), cache),
            pl.BlockSpec((bb, WINDOW, KV_WIDTH), cache),
            pl.BlockSpec(memory_space=pltpu.SMEM),
            pl.BlockSpec((Q_WIDTH, D_MODEL), lambda i: (0, 0)),
        ],
        out_specs=[
            pl.BlockSpec((bb * t, D_MODEL), row),
            pl.BlockSpec((bb, WINDOW, KV_WIDTH), cache),
            pl.BlockSpec((bb, WINDOW, KV_WIDTH), cache),
        ],
        out_shape=[
            jax.ShapeDtypeStruct((n, D_MODEL), F32),
            jax.ShapeDtypeStruct((nb, WINDOW, KV_WIDTH), F32),
            jax.ShapeDtypeStruct((nb, WINDOW, KV_WIDTH), F32),
        ],
        compiler_params=_cparams(1),
        name="attn_sample",
    )(x2, q, kn, vn, ck, cv, sinks, wo)


_CAND_LIMITS = tuple(PEER_TOPK // (r2 + 1) for r2 in range(1, SUBLANES))


def _sorting_network(n):
    pairs = []
    p = 1
    while p < n:
        k = p
        while k >= 1:
            for j in range(k % p, n - k, 2 * k):
                for i in range(min(k, n - j - k)):
                    if (i + j) // (2 * p) == (i + j + k) // (2 * p):
                        pairs.append((i + j, i + j + k))
            k //= 2
        p *= 2
    return tuple(pairs)


_SORT_PAIRS = _sorting_network(PEER_TOPK)


def _compare_exchange(x, i, j):
    if x[j] is not None:
        x[i], x[j] = jnp.maximum(x[i], x[j]), jnp.minimum(x[i], x[j])


def _top_sorted(tiles, smallest_only=False):
    n = len(tiles)
    x = list(tiles) + [None] * (PEER_TOPK - n)
    for i, j in _SORT_PAIRS:
        if j < n:
            _compare_exchange(x, i, j)
    for shift in (4, 2, 1):
        y = [None if v is None else pltpu.roll(v, shift, 0) for v in x]
        m = []
        for k in range(PEER_TOPK):
            a, b = x[k], y[PEER_TOPK - 1 - k]
            m.append(b if a is None else a if b is None else jnp.maximum(a, b))
        if smallest_only and shift == 1:
            while len(m) > 1:
                m = [jnp.minimum(m[2 * i], m[2 * i + 1]) for i in range(len(m) // 2)]
            return m[0]
        d = PEER_TOPK // 2
        while d:
            for k in range(PEER_TOPK):
                if not k & d:
                    _compare_exchange(m, k, k + d)
            d //= 2
        x = m
    return x


def _rank_bits(v, s):
    b8 = v[7] > s
    b4 = jnp.where(b8, v[11], v[3]) > s
    b2 = jnp.where(b8, jnp.where(b4, v[13], v[9]), jnp.where(b4, v[5], v[1])) > s
    piv = [jnp.where(b2, v[4 * i + 2], v[4 * i]) for i in range(4)]
    piv = [jnp.where(b4, piv[2 * i + 1], piv[2 * i]) for i in range(2)]
    b1 = jnp.where(b8, piv[1], piv[0]) > s
    b0 = v[PEER_TOPK - 1] > s
    return b8, b4, b2, b1, b0


def _peer_select(s1, s2, thr_s):
    nt8 = PEER_NKEYS // SUBLANES
    s1 = s1.reshape(nt8, SUBLANES, LANES)
    s2 = s2.reshape(nt8, SUBLANES, LANES)
    v1 = _top_sorted([s1[i] for i in range(nt8)])
    v2 = _top_sorted([s2[i] for i in range(nt8)])
    row8 = lax.broadcasted_iota(jnp.int32, (SUBLANES, LANES), 0)

    def column(v, start):
        col = v[start]
        for s in range(1, SUBLANES):
            col = jnp.where(row8 == s, v[start + s], col)
        return col

    v1_lo, v1_hi, v2_hi = column(v1, 0), column(v1, SUBLANES), column(v2, SUBLANES)
    cand = [v1_lo + v2[0], v1_hi + v2[0]]
    for r2, lim in enumerate(_CAND_LIMITS, start=1):
        g = v1_lo + v2[r2]
        cand.append(jnp.where(row8 < lim, g, NEG_INF) if lim < SUBLANES else g)
    cand.append(v2_hi + v1[0])
    tau = _top_sorted(cand, smallest_only=True)
    best = v1[0] + v2[0]
    sel = [c >= tau for c in cand]
    zt = jnp.zeros((SUBLANES, LANES), F32)
    for c, m in zip(cand, sel):
        zt = zt + jnp.where(m, jnp.exp(c - best), 0.0)
    z = jnp.sum(zt, axis=0, keepdims=True)
    self32 = [m.astype(F32) for m in sel]
    lo = self32[0]
    for t in self32[2:-1]:
        lo = lo + t
    tail = jnp.sum(self32[-1], axis=0, keepdims=True)
    counts = (lo + jnp.where(row8 == 0, tail, 0.0), self32[1])
    for half, cnt in enumerate(counts):
        thr = jnp.full((SUBLANES, LANES), POS_INF, F32)
        for k in range(PEER_TOPK):
            thr = jnp.where(cnt == k + 1.0, v2[k], thr)
        thr_s[half * SUBLANES:(half + 1) * SUBLANES, :] = thr
    bits1 = _rank_bits(v1, s1)
    level = [thr_s[r:r + 1, :] for r in range(PEER_TOPK)]
    for bit in bits1[3::-1]:
        level = [jnp.where(bit, level[2 * i + 1], level[2 * i]) for i in range(len(level) // 2)]
    thr1 = jnp.where(bits1[4], POS_INF, level[0])
    e2 = jnp.exp(s2 - v2[0])
    e1h = jnp.exp(s1 - v1[0]) * (0.5 / z)
    return e2, thr1, e1h


def _peer_kernel(x_ref, g_ref, wq_ref, sk_ref, u_ref, v_ref, o_ref,
                 hn_s, q_s, s_s, thr_s, e2_s, t1_s, e1_s, a0_s, a1_s, w_s, y_s,
                 *, tm, te, tc, kb, nt):
    j = pl.program_id(1)
    nlc = tm // LANES
    gi = te // PEER_NKEYS
    tiles = (PEER_NKEYS // SUBLANES, SUBLANES, LANES)

    @pl.when(j == 0)
    def _():
        hn = _rmsnorm(x_ref[...], g_ref[...]).astype(BF16)
        hn_s[...] = hn
        q_s[...] = _dot(hn, wq_ref[...]).astype(BF16)
        half = PEER_DKEY // 2
        for h in range(PEER_HEADS):
            for p in range(2):
                q_hp = q_s[:, (2 * h + p) * half:(2 * h + p + 1) * half]
                st = _dot_nt(sk_ref[h, p], q_hp)
                for ck in range(nlc):
                    s_s[2 * h + p, ck] = st[:, ck * LANES:(ck + 1) * LANES]

        def select(idx, carry):
            h = idx // nlc
            ck = idx % nlc
            e2, thr1, e1h = _peer_select(s_s[2 * h, ck], s_s[2 * h + 1, ck], thr_s)
            flat = (PEER_NKEYS, LANES)
            e2_s[h, ck] = e2.reshape(flat)
            t1_s[h, ck] = thr1.reshape(flat)
            e1_s[h, ck] = e1h.reshape(flat)
            return carry

        lax.fori_loop(0, PEER_HEADS * nlc, select, 0)
        y_s[...] = jnp.zeros_like(y_s)

    i1base = jnp.maximum(j - 1, 0) * gi
    per_kb = kb // PEER_NKEYS

    def step(a_w, a_r):
        for c in range(tm // tc if a_w is not None else 0):
            tok = slice(c * tc, (c + 1) * tc)
            a_w[c] = _dot_nt(u_ref[...], hn_s[tok, :])
        for b in range(te // kb if a_r is not None else 0):
            grp, l0 = divmod(b * per_kb, SUBLANES)
            i1s = pl.ds(pl.multiple_of(i1base + grp * SUBLANES, SUBLANES), SUBLANES)
            for ck in range(nlc):
                c, lc = divmod(ck, tc // LANES)
                ls = slice(lc * LANES, (lc + 1) * LANES)
                thr = [t1_s[h, ck, i1s, :] for h in range(PEER_HEADS)]
                e1 = [e1_s[h, ck, i1s, :] for h in range(PEER_HEADS)]
                rows = range(l0, l0 + per_kb)
                accs = [jnp.zeros(tiles, F32) for _ in rows]
                for h in range(PEER_HEADS):
                    s2 = s_s[2 * h + 1, ck].reshape(tiles)
                    e2 = e2_s[h, ck].reshape(tiles)
                    for n, l in enumerate(rows):
                        tb = jnp.broadcast_to(thr[h][l:l + 1], (SUBLANES, LANES))
                        eb = jnp.broadcast_to(e1[h][l:l + 1], (SUBLANES, LANES))
                        accs[n] = accs[n] + jnp.where(s2 >= tb, e2, 0.0) * eb
                for acc, l in zip(accs, rows):
                    rs = slice((grp * SUBLANES + l) * PEER_NKEYS, (grp * SUBLANES + l + 1) * PEER_NKEYS)
                    a = a_r[c, rs, ls].reshape(tiles)
                    gelu2 = a * (1.0 + lax.erf(a * math.sqrt(0.5)))
                    w = (acc * gelu2).reshape(PEER_NKEYS, LANES).astype(BF16)
                    w_s[ck * LANES:(ck + 1) * LANES, rs] = w.T
            es = slice(b * kb, (b + 1) * kb)
            y_s[...] += _dot(w_s[:, es], v_ref[es, :])

    inner = (j > 0) & (j < nt)

    @pl.when(j == 0)
    def _():
        step(a0_s, None)

    @pl.when(inner & (j % 2 == 0))
    def _():
        step(a0_s, a1_s)

    @pl.when(inner & (j % 2 == 1))
    def _():
        step(a1_s, a0_s)

    @pl.when(j == nt)
    def _():
        step(None, a1_s if nt % 2 == 0 else a0_s)
        o_ref[...] = x_ref[...] + y_s[...]


def _peer(x2, g, wq, sk, u_tab, v_tab, layer, tm=512, te=2048, tc=256, kb=256):
    n = x2.shape[0]
    nt = u_tab.shape[1] // te
    nlc = tm // LANES
    return pl.pallas_call(
        functools.partial(_peer_kernel, tm=tm, te=te, tc=tc, kb=kb, nt=nt),
        grid=(n // tm, nt + 1),
        in_specs=[
            pl.BlockSpec((tm, D_MODEL), lambda i, j: (i, 0)),
            pl.BlockSpec((1, D_MODEL), lambda i, j: (0, 0)),
            pl.BlockSpec((None,) + wq.shape[1:], lambda i, j: (layer, 0, 0), pipeline_mode=pl.Buffered(1)),
            pl.BlockSpec((None,) + sk.shape[1:], lambda i, j: (layer, 0, 0, 0, 0), pipeline_mode=pl.Buffered(1)),
            pl.BlockSpec((None, te, D_MODEL), lambda i, j: (layer, jnp.minimum(j, nt - 1), 0)),
            pl.BlockSpec((None, te, D_MODEL), lambda i, j: (layer, jnp.maximum(j - 1, 0), 0)),
        ],
        out_specs=pl.BlockSpec((tm, D_MODEL), lambda i, j: (i, 0)),
        out_shape=jax.ShapeDtypeStruct((n, D_MODEL), F32),
        scratch_shapes=[
            pltpu.VMEM((tm, D_MODEL), BF16),
            pltpu.VMEM((tm, PEER_HEADS * PEER_DKEY), BF16),
            pltpu.VMEM((2 * PEER_HEADS, nlc, PEER_NKEYS, LANES), F32),
            pltpu.VMEM((PEER_TOPK, LANES), F32),
            pltpu.VMEM((PEER_HEADS, nlc, PEER_NKEYS, LANES), F32),
            pltpu.VMEM((PEER_HEADS, nlc, PEER_NKEYS, LANES), F32),
            pltpu.VMEM((PEER_HEADS, nlc, PEER_NKEYS, LANES), F32),
            pltpu.VMEM((tm // tc, te, tc), F32),
            pltpu.VMEM((tm // tc, te, tc), F32),
            pltpu.VMEM((tm, te), BF16),
            pltpu.VMEM((tm, D_MODEL), F32),
        ],
        compiler_params=_cparams(2),
        name="peer",
    )(x2, g, wq, sk, u_tab, v_tab)


def _rope_tables(pos):
    half = HEAD_DIM // 2
    inv_freq = jnp.power(jnp.float32(ROPE_THETA), -jnp.arange(half, dtype=F32) * (2.0 / HEAD_DIM))
    ang = pos.astype(F32)[:, None] * inv_freq[None, :]
    cos = jnp.concatenate([jnp.cos(ang)] * 4, axis=1)
    sin = jnp.concatenate([jnp.sin(ang)] * 4, axis=1)
    return cos, sin


def kernel(x_prompt, x_sample, state_pool, cache_k, cache_v, norm_mix, norm_ffn, pool_w_in, pool_w_group,
           pool_scale, attn_w_qkv, attn_q_norm, attn_k_norm, attn_sinks, attn_w_o, peer_w_q, peer_subkeys,
           peer_u, peer_v):
    b, s, _ = x_prompt.shape
    db, t, _ = x_sample.shape
    depth = norm_mix.shape[0]
    wq_all = peer_w_q.astype(BF16)
    sk_all = peer_subkeys.astype(BF16)
    u_all = peer_u.astype(BF16)
    v_all = peer_v.astype(BF16)
    xp = x_prompt
    xs = x_sample.reshape(db * t, D_MODEL)
    seg = jnp.asarray(np.kron(np.eye(256 // HEAD_DIM), np.ones((HEAD_DIM, HEAD_DIM))), BF16)
    cos_p, sin_p = _rope_tables(jnp.arange(s, dtype=jnp.int32))
    cos_s, sin_s = _rope_tables(PAST_LEN + jnp.arange(t, dtype=jnp.int32))
    pool_p, pool_s, kp_l, vp_l, ks_l, vs_l = [], [], [], [], [], []
    for i in range(depth):
        jl = i // 2
        g_mix = norm_mix[i][None, :]
        if i % 2 == 0:
            win = pool_w_in[jl].astype(BF16)
            wg = pool_w_group[jl].astype(BF16)
            sc = pool_scale[jl][None, :]
            xp, st_p = _pool_prompt(xp, g_mix, win, wg, sc)
            prior16 = jnp.pad(state_pool[jl], ((0, 0), (1, 0), (0, 0)))
            xs, st_s = _pool_sample(xs, prior16, g_mix, win, wg, sc, t)
            pool_p.append(st_p[:, 1:])
            pool_s.append(st_s[:, 1:])
        else:
            wqkv = attn_w_qkv[jl].astype(BF16)
            wo = attn_w_o[jl].astype(BF16)
            gain = jnp.concatenate([jnp.tile(attn_q_norm[jl], N_HEADS), jnp.tile(attn_k_norm[jl], N_KV_HEADS)])[None, :]
            sinks = attn_sinks[jl]
            tq = 512
            qp, kp, vp = _qkv(xp.reshape(b * s, D_MODEL), g_mix, wqkv, seg, gain, cos_p, sin_p, tq, s // tq)
            kp3 = kp.reshape(b, s, KV_WIDTH)
            vp3 = vp.reshape(b, s, KV_WIDTH)
            xp = _attn_prompt(xp, qp.reshape(b, s, Q_WIDTH), kp3, vp3, sinks, wo)
            cos_st = jnp.tile(cos_s, (tq // t, 1))
            sin_st = jnp.tile(sin_s, (tq // t, 1))
            qs, kq, vq = _qkv(xs, g_mix, wqkv, seg, gain, cos_st, sin_st, tq, 1)
            ck = cache_k[jl].reshape(db, WINDOW, KV_WIDTH)
            cv = cache_v[jl].reshape(db, WINDOW, KV_WIDTH)
            xs, nk_s, nv_s = _attn_sample(xs, qs, kq, vq, ck, cv, sinks, wo, t)
            kp_l.append(kp3[:, -WINDOW:].reshape(b, WINDOW, N_KV_HEADS, HEAD_DIM))
            vp_l.append(vp3[:, -WINDOW:].reshape(b, WINDOW, N_KV_HEADS, HEAD_DIM))
            ks_l.append(nk_s.reshape(db, WINDOW, N_KV_HEADS, HEAD_DIM))
            vs_l.append(nv_s.reshape(db, WINDOW, N_KV_HEADS, HEAD_DIM))
        g_ffn = norm_ffn[i][None, :]
        xp = _peer(xp.reshape(b * s, D_MODEL), g_ffn, wq_all, sk_all, u_all, v_all, i).reshape(b, s, D_MODEL)
        xs = _peer(xs, g_ffn, wq_all, sk_all, u_all, v_all, i)
    return (xp, xs.reshape(db, t, D_MODEL), jnp.stack(pool_p), jnp.stack(pool_s), jnp.stack(kp_l),
            jnp.stack(vp_l), jnp.stack(ks_l), jnp.stack(vs_l))
```

```python
import functools
import math

import jax
import jax.numpy as jnp
import numpy as np
from jax import lax
from jax.experimental import pallas as pl
from jax.experimental.pallas import tpu as pltpu

F32 = jnp.float32
BF16 = jnp.bfloat16

D_MODEL = 1024
LANES = 128
SUBLANES = 8
POS_INF = float("inf")
NORM_EPS = 1e-6
NEG_INF = float("-inf")

POOL_WINDOWS = (2, 4, 8, 16)
POOL_GROUP_DIM = D_MODEL // len(POOL_WINDOWS)
POOL_HIST = 16

HEAD_DIM = 64
N_HEADS = 16
N_KV_HEADS = 4
GQA_GROUP = N_HEADS // N_KV_HEADS
WINDOW = 128
PAST_LEN = 16384
ROPE_THETA = 10000.0
ATTN_SCALE = 1.0 / math.sqrt(HEAD_DIM)
Q_WIDTH = N_HEADS * HEAD_DIM
KV_WIDTH = N_KV_HEADS * HEAD_DIM
MASK_NEG = -1e30

PEER_HEADS = 8
PEER_NKEYS = 128
PEER_TOPK = 16
PEER_DKEY = 256

VMEM_LIMIT = 60 * 1024 * 1024


def _cparams(n_axes, **kw):
    return pltpu.CompilerParams(dimension_semantics=("arbitrary",) * n_axes,
                                vmem_limit_bytes=VMEM_LIMIT, **kw)


def _rmsnorm(x, g):
    return x * lax.rsqrt(jnp.mean(x * x, axis=-1, keepdims=True) + NORM_EPS) * g


def _dot(a, b):
    return jnp.dot(a, b, preferred_element_type=F32)


def _dot_nt(a, b):
    return lax.dot_general(a, b, (((1,), (1,)), ((), ())), preferred_element_type=F32)


def _pool_prompt_kernel(x_ref, g_ref, win_ref, wg_ref, sc_ref, y_ref, st_ref, ext_s, lvl_a, lvl_b, *, ts):
    sb = pl.program_id(1)
    base = SUBLANES
    cur = SUBLANES + POOL_HIST
    nrow = POOL_HIST + ts

    @pl.when(sb == 0)
    def _():
        zeros = jnp.zeros((cur, D_MODEL), F32)
        ext_s[0:cur, :] = zeros
        lvl_a[0:cur, :] = zeros
        lvl_b[0:cur, :] = zeros

    x = x_ref[0]
    h = _rmsnorm(x, g_ref[...])
    u = _dot(h.astype(BF16), win_ref[...])
    ext_s[cur:cur + ts, :] = u

    def doubled(src, shift, c0):
        return src[base:base + nrow, c0:] + src[base - shift:base - shift + nrow, c0:]

    gd = POOL_GROUP_DIM
    lvl_a[base:base + nrow, :] = doubled(ext_s, 1, 0)
    lvl_b[base:base + nrow, gd:] = doubled(lvl_a, 2, gd)
    lvl_a[base:base + nrow, 2 * gd:] = doubled(lvl_b, 4, 2 * gd)
    sums = [lvl_a[cur:cur + ts, 0:gd], lvl_b[cur:cur + ts, gd:2 * gd], lvl_a[cur:cur + ts, 2 * gd:3 * gd],
            lvl_a[cur:cur + ts, 3 * gd:] + lvl_a[cur - 8:cur - 8 + ts, 3 * gd:]]
    pos = sb * ts + lax.broadcasted_iota(jnp.int32, (ts, 1), 0)
    outs = []
    for g, w in enumerate(POOL_WINDOWS):
        ug = u[:, g * gd:(g + 1) * gd]
        inv_count = 1.0 / jnp.minimum(pos + 1, w).astype(F32)
        pooled = sums[g] * inv_count - ug
        outs.append(_dot(pooled.astype(BF16), wg_ref[g]))
    y = jnp.concatenate(outs, axis=1) * sc_ref[...]
    y_ref[0] = x + y
    st_ref[0] = u[ts - POOL_HIST:ts, :]
    ext_s[base:cur, :] = u[ts - POOL_HIST:ts, :]


def _pool_prompt(x, g, win, wg, sc, ts=512):
    b, s, _ = x.shape
    return pl.pallas_call(
        functools.partial(_pool_prompt_kernel, ts=ts),
        grid=(b, s // ts),
        in_specs=[
            pl.BlockSpec((1, ts, D_MODEL), lambda i, j: (i, j, 0)),
            pl.BlockSpec((1, D_MODEL), lambda i, j: (0, 0)),
            pl.BlockSpec((D_MODEL, D_MODEL), lambda i, j: (0, 0)),
            pl.BlockSpec((len(POOL_WINDOWS), POOL_GROUP_DIM, POOL_GROUP_DIM), lambda i, j: (0, 0, 0)),
            pl.BlockSpec((1, D_MODEL), lambda i, j: (0, 0)),
        ],
        out_specs=[
            pl.BlockSpec((1, ts, D_MODEL), lambda i, j: (i, j, 0)),
            pl.BlockSpec((1, POOL_HIST, D_MODEL), lambda i, j: (i, 0, 0)),
        ],
        out_shape=[
            jax.ShapeDtypeStruct((b, s, D_MODEL), F32),
            jax.ShapeDtypeStruct((b, POOL_HIST, D_MODEL), F32),
        ],
        scratch_shapes=[pltpu.VMEM((SUBLANES + POOL_HIST + ts, D_MODEL), F32)] * 3,
        compiler_params=_cparams(2),
        name="pool_prompt",
    )(x, g, win, wg, sc)


def _pool_sample_kernel(x_ref, pr_ref, g_ref, win_ref, wg_ref, sc_ref, y_ref, st_ref, ext_s, *, bb, t):
    x = x_ref[...]
    h = _rmsnorm(x, g_ref[...])
    u = _dot(h.astype(BF16), win_ref[...])
    ext_s[:, 0:POOL_HIST, :] = pr_ref[...]
    ext_s[:, POOL_HIST:POOL_HIST + t, :] = u.reshape(bb, t, D_MODEL)
    outs = []
    for g, w in enumerate(POOL_WINDOWS):
        c0 = g * POOL_GROUP_DIM
        win_sum = ext_s[:, POOL_HIST:POOL_HIST + t, c0:c0 + POOL_GROUP_DIM]
        for j in range(1, w):
            win_sum = win_sum + ext_s[:, POOL_HIST - j:POOL_HIST - j + t, c0:c0 + POOL_GROUP_DIM]
        pooled = win_sum.reshape(bb * t, POOL_GROUP_DIM) / float(w) - u[:, c0:c0 + POOL_GROUP_DIM]
        outs.append(_dot(pooled.astype(BF16), wg_ref[g]))
    y = jnp.concatenate(outs, axis=1) * sc_ref[...]
    y_ref[...] = x + y
    st_ref[...] = ext_s[:, t:t + POOL_HIST, :]


def _pool_sample(x2, prior16, g, win, wg, sc, t, bb=32):
    n = x2.shape[0]
    nb = n // t
    return pl.pallas_call(
        functools.partial(_pool_sample_kernel, bb=bb, t=t),
        grid=(nb // bb,),
        in_specs=[
            pl.BlockSpec((bb * t, D_MODEL), lambda i: (i, 0)),
            pl.BlockSpec((bb, POOL_HIST, D_MODEL), lambda i: (i, 0, 0)),
            pl.BlockSpec((1, D_MODEL), lambda i: (0, 0)),
            pl.BlockSpec((D_MODEL, D_MODEL), lambda i: (0, 0)),
            pl.BlockSpec((len(POOL_WINDOWS), POOL_GROUP_DIM, POOL_GROUP_DIM), lambda i: (0, 0, 0)),
            pl.BlockSpec((1, D_MODEL), lambda i: (0, 0)),
        ],
        out_specs=[
            pl.BlockSpec((bb * t, D_MODEL), lambda i: (i, 0)),
            pl.BlockSpec((bb, POOL_HIST, D_MODEL), lambda i: (i, 0, 0)),
        ],
        out_shape=[
            jax.ShapeDtypeStruct((n, D_MODEL), F32),
            jax.ShapeDtypeStruct((nb, POOL_HIST, D_MODEL), F32),
        ],
        scratch_shapes=[pltpu.VMEM((bb, POOL_HIST + t, D_MODEL), F32)],
        compiler_params=_cparams(1),
        name="pool_sample",
    )(x2, prior16, g, win, wg, sc)


def _qkv_kernel(x_ref, g_ref, w_ref, seg_ref, gain_ref, cos_ref, sin_ref, q_ref, k_ref, v_ref):
    x = x_ref[...]
    h = _rmsnorm(x, g_ref[...])
    qkv = _dot(h.astype(BF16), w_ref[...])
    nqk = Q_WIDTH + KV_WIDTH
    qk = qkv[:, :nqk]
    sq = qk * qk
    sq_hi = sq.astype(BF16)
    sq_lo = (sq - sq_hi.astype(F32)).astype(BF16)
    seg = seg_ref[...]
    cw = seg.shape[0]
    parts = []
    for c in range(nqk // cw):
        sl = slice(c * cw, (c + 1) * cw)
        parts.append(_dot(sq_hi[:, sl], seg) + _dot(sq_lo[:, sl], seg))
    ssq = jnp.concatenate(parts, axis=1)
    qk = qk * lax.rsqrt(ssq * (1.0 / HEAD_DIM) + NORM_EPS) * gain_ref[...]
    half = HEAD_DIM // 2
    lane = lax.broadcasted_iota(jnp.int32, qk.shape, 1)
    first = (lane % HEAD_DIM) < half
    up = pltpu.roll(qk, nqk - half, 1)
    down = pltpu.roll(qk, half, 1)
    rot = jnp.where(first, -up, down)
    reps = nqk // cos_ref.shape[1]
    cos = jnp.concatenate([cos_ref[...]] * reps, axis=1)
    sin = jnp.concatenate([sin_ref[...]] * reps, axis=1)
    qk = qk * cos + rot * sin
    q_ref[...] = qk[:, :Q_WIDTH]
    k_ref[...] = qk[:, Q_WIDTH:]
    v_ref[...] = qkv[:, nqk:]


def _qkv(x2, g, w, seg, gain, cos, sin, tq, pos_blocks):
    n = x2.shape[0]
    tw = cos.shape[1]
    return pl.pallas_call(
        _qkv_kernel,
        grid=(n // tq,),
        in_specs=[
            pl.BlockSpec((tq, D_MODEL), lambda i: (i, 0)),
            pl.BlockSpec((1, D_MODEL), lambda i: (0, 0)),
            pl.BlockSpec(w.shape, lambda i: (0, 0)),
            pl.BlockSpec(seg.shape, lambda i: (0, 0)),
            pl.BlockSpec((1, Q_WIDTH + KV_WIDTH), lambda i: (0, 0)),
            pl.BlockSpec((tq, tw), lambda i: (i % pos_blocks, 0)),
            pl.BlockSpec((tq, tw), lambda i: (i % pos_blocks, 0)),
        ],
        out_specs=[
            pl.BlockSpec((tq, Q_WIDTH), lambda i: (i, 0)),
            pl.BlockSpec((tq, KV_WIDTH), lambda i: (i, 0)),
            pl.BlockSpec((tq, KV_WIDTH), lambda i: (i, 0)),
        ],
        out_shape=[
            jax.ShapeDtypeStruct((n, Q_WIDTH), F32),
            jax.ShapeDtypeStruct((n, KV_WIDTH), F32),
            jax.ShapeDtypeStruct((n, KV_WIDTH), F32),
        ],
        compiler_params=_cparams(1),
        name="attn_qkv",
    )(x2, g, w, seg, gain, cos, sin)


def _attn_prompt_kernel(x_ref, q_ref, kc_ref, vc_ref, sink_ref, wo_ref, y_ref, kp_s, vp_s, *, nsub):
    nb0 = pl.program_id(1) * nsub

    @pl.when(pl.program_id(1) == 0)
    def _():
        kp_s[...] = jnp.zeros_like(kp_s)
        vp_s[...] = jnp.zeros_like(vp_s)

    q_all = q_ref[0].astype(BF16)
    k_all = jnp.concatenate([kp_s[...], kc_ref[0]], axis=0).astype(BF16)
    v_all = jnp.concatenate([vp_s[...], vc_ref[0]], axis=0).astype(BF16)
    qi = lax.broadcasted_iota(jnp.int32, (WINDOW, 2 * WINDOW), 0)
    kj = lax.broadcasted_iota(jnp.int32, (WINDOW, 2 * WINDOW), 1)
    band = (kj > qi) & (kj <= qi + WINDOW)
    o_blocks = []
    for sub in range(nsub):
        q = q_all[sub * WINDOW:(sub + 1) * WINDOW]
        k2 = k_all[sub * WINDOW:(sub + 2) * WINDOW]
        v2 = v_all[sub * WINDOW:(sub + 2) * WINDOW]
        mask = band & ((nb0 + sub) * WINDOW - WINDOW + kj >= 0)
        outs = []
        for head in range(N_HEADS):
            kh = head // GQA_GROUP
            kk = k2[:, kh * HEAD_DIM:(kh + 1) * HEAD_DIM]
            vv = v2[:, kh * HEAD_DIM:(kh + 1) * HEAD_DIM]
            sk = sink_ref[head]
            s = _dot_nt(q[:, head * HEAD_DIM:(head + 1) * HEAD_DIM], kk) * ATTN_SCALE
            s = jnp.where(mask, s, MASK_NEG)
            m = jnp.maximum(jnp.max(s, axis=-1, keepdims=True), sk)
            p = jnp.exp(s - m)
            p = p / (jnp.sum(p, axis=-1, keepdims=True) + jnp.exp(sk - m))
            outs.append(_dot(p.astype(BF16), vv))
        o_blocks.append(jnp.concatenate(outs, axis=1))
    o = jnp.concatenate(o_blocks, axis=0)
    y_ref[0] = x_ref[0] + _dot(o.astype(BF16), wo_ref[...])
    kp_s[...] = kc_ref[0, (nsub - 1) * WINDOW:, :]
    vp_s[...] = vc_ref[0, (nsub - 1) * WINDOW:, :]


def _attn_prompt(x, q, k, v, sinks, wo, nsub=4):
    b, s, _ = x.shape
    tq = nsub * WINDOW
    cur = lambda i, j: (i, j, 0)
    return pl.pallas_call(
        functools.partial(_attn_prompt_kernel, nsub=nsub),
        grid=(b, s // tq),
        in_specs=[
            pl.BlockSpec((1, tq, D_MODEL), cur),
            pl.BlockSpec((1, tq, Q_WIDTH), cur),
            pl.BlockSpec((1, tq, KV_WIDTH), cur),
            pl.BlockSpec((1, tq, KV_WIDTH), cur),
            pl.BlockSpec(memory_space=pltpu.SMEM),
            pl.BlockSpec((Q_WIDTH, D_MODEL), lambda i, j: (0, 0)),
        ],
        out_specs=pl.BlockSpec((1, tq, D_MODEL), cur),
        out_shape=jax.ShapeDtypeStruct((b, s, D_MODEL), F32),
        scratch_shapes=[pltpu.VMEM((WINDOW, KV_WIDTH), F32), pltpu.VMEM((WINDOW, KV_WIDTH), F32)],
        compiler_params=_cparams(2),
        name="attn_prompt",
    )(x, q, k, v, sinks, wo)


def _attn_sample_kernel(x_ref, q_ref, kn_ref, vn_ref, ck_ref, cv_ref, sink_ref, wo_ref,
                        y_ref, nk_ref, nv_ref, *, bb, t):
    q = q_ref[...].reshape(bb, t, Q_WIDTH)
    kn = kn_ref[...].reshape(bb, t, KV_WIDTH)
    vn = vn_ref[...].reshape(bb, t, KV_WIDTH)
    ck = ck_ref[...]
    cv = cv_ref[...]
    nk_ref[:, 0:WINDOW - t, :] = ck[:, t:, :]
    nk_ref[:, WINDOW - t:, :] = kn
    nv_ref[:, 0:WINDOW - t, :] = cv[:, t:, :]
    nv_ref[:, WINDOW - t:, :] = vn
    rows = GQA_GROUP * t
    qi_c = lax.broadcasted_iota(jnp.int32, (bb, rows, WINDOW), 1) % t
    kj_c = lax.broadcasted_iota(jnp.int32, (bb, rows, WINDOW), 2)
    mask_c = kj_c > qi_c
    qi_n = lax.broadcasted_iota(jnp.int32, (bb, rows, t), 1) % t
    kj_n = lax.broadcasted_iota(jnp.int32, (bb, rows, t), 2)
    mask_n = kj_n <= qi_n
    ckb, cvb, knb, vnb = ck.astype(BF16), cv.astype(BF16), kn.astype(BF16), vn.astype(BF16)
    outs = []
    for kh in range(N_KV_HEADS):
        hs = slice(kh * HEAD_DIM, (kh + 1) * HEAD_DIM)
        q4 = jnp.concatenate(
            [q[:, :, (kh * GQA_GROUP + g) * HEAD_DIM:(kh * GQA_GROUP + g + 1) * HEAD_DIM] for g in range(GQA_GROUP)],
            axis=1).astype(BF16)
        sk = jnp.concatenate(
            [jnp.full((1, t, 1), 1.0, F32) * sink_ref[kh * GQA_GROUP + g] for g in range(GQA_GROUP)], axis=1)
        s_c = jnp.einsum("bqd,bkd->bqk", q4, ckb[:, :, hs], preferred_element_type=F32) * ATTN_SCALE
        s_n = jnp.einsum("bqd,bkd->bqk", q4, knb[:, :, hs], preferred_element_type=F32) * ATTN_SCALE
        s_c = jnp.where(mask_c, s_c, MASK_NEG)
        s_n = jnp.where(mask_n, s_n, MASK_NEG)
        m = jnp.maximum(jnp.maximum(jnp.max(s_c, axis=-1, keepdims=True),
                                    jnp.max(s_n, axis=-1, keepdims=True)), sk)
        p_c = jnp.exp(s_c - m)
        p_n = jnp.exp(s_n - m)
        den = jnp.sum(p_c, axis=-1, keepdims=True) + jnp.sum(p_n, axis=-1, keepdims=True) + jnp.exp(sk - m)
        inv = 1.0 / den
        o4 = (jnp.einsum("bqk,bkd->bqd", (p_c * inv).astype(BF16), cvb[:, :, hs], preferred_element_type=F32)
              + jnp.einsum("bqk,bkd->bqd", (p_n * inv).astype(BF16), vnb[:, :, hs], preferred_element_type=F32))
        for g in range(GQA_GROUP):
            outs.append(o4[:, g * t:(g + 1) * t, :])
    o = jnp.concatenate(outs, axis=2).reshape(bb * t, Q_WIDTH)
    y_ref[...] = x_ref[...] + _dot(o.astype(BF16), wo_ref[...])


def _attn_sample(x2, q, kn, vn, ck, cv, sinks, wo, t, bb=16):
    n = x2.shape[0]
    nb = n // t
    row = lambda i: (i, 0)
    cache = lambda i: (i, 0, 0)
    return pl.pallas_call(
        functools.partial(_attn_sample_kernel, bb=bb, t=t),
        grid=(nb // bb,),
        in_specs=[
            pl.BlockSpec((bb * t, D_MODEL), row),
            pl.BlockSpec((bb * t, Q_WIDTH), row),
            pl.BlockSpec((bb * t, KV_WIDTH), row),
            pl.BlockSpec((bb * t, KV_WIDTH), row),
            pl.BlockSpec((bb, WINDOW, KV_WIDTH), cache),
            pl.BlockSpec((bb, WINDOW, KV_WIDTH), cache),
            pl.BlockSpec(memory_space=pltpu.SMEM),
            pl.BlockSpec((Q_WIDTH, D_MODEL), lambda i: (0, 0)),
        ],
        out_specs=[
            pl.BlockSpec((bb * t, D_MODEL), row),
            pl.BlockSpec((bb, WINDOW, KV_WIDTH), cache),
            pl.BlockSpec((bb, WINDOW, KV_WIDTH), cache),
        ],
        out_shape=[
            jax.ShapeDtypeStruct((n, D_MODEL), F32),
            jax.ShapeDtypeStruct((nb, WINDOW, KV_WIDTH), F32),
            jax.ShapeDtypeStruct((nb, WINDOW, KV_WIDTH), F32),
        ],
        compiler_params=_cparams(1),
        name="attn_sample",
    )(x2, q, kn, vn, ck, cv, sinks, wo)


_CAND_LIMITS = tuple(PEER_TOPK // (r2 + 1) for r2 in range(1, SUBLANES))


def _sorting_network(n):
    pairs = []
    p = 1
    while p < n:
        k = p
        while k >= 1:
            for j in range(k % p, n - k, 2 * k):
                for i in range(min(k, n - j - k)):
                    if (i + j) // (2 * p) == (i + j + k) // (2 * p):
                        pairs.append((i + j, i + j + k))
            k //= 2
        p *= 2
    return tuple(pairs)


_SORT_PAIRS = _sorting_network(PEER_TOPK)


def _compare_exchange(x, i, j):
    if x[j] is not None:
        x[i], x[j] = jnp.maximum(x[i], x[j]), jnp.minimum(x[i], x[j])


def _top_sorted(tiles, smallest_only=False):
    n = len(tiles)
    x = list(tiles) + [None] * (PEER_TOPK - n)
    for i, j in _SORT_PAIRS:
        if j < n:
            _compare_exchange(x, i, j)
    for shift in (4, 2, 1):
        y = [None if v is None else pltpu.roll(v, shift, 0) for v in x]
        m = []
        for k in range(PEER_TOPK):
            a, b = x[k], y[PEER_TOPK - 1 - k]
            m.append(b if a is None else a if b is None else jnp.maximum(a, b))
        if smallest_only and shift == 1:
            while len(m) > 1:
                m = [jnp.minimum(m[2 * i], m[2 * i + 1]) for i in range(len(m) // 2)]
            return m[0]
        d = PEER_TOPK // 2
        while d:
            for k in range(PEER_TOPK):
                if not k & d:
                    _compare_exchange(m, k, k + d)
            d //= 2
        x = m
    return x


def _rank_bits(v, s):
    b8 = v[7] > s
    b4 = jnp.where(b8, v[11], v[3]) > s
    b2 = jnp.where(b8, jnp.where(b4, v[13], v[9]), jnp.where(b4, v[5], v[1])) > s
    piv = [jnp.where(b2, v[4 * i + 2], v[4 * i]) for i in range(4)]
    piv = [jnp.where(b4, piv[2 * i + 1], piv[2 * i]) for i in range(2)]
    b1 = jnp.where(b8, piv[1], piv[0]) > s
    b0 = v[PEER_TOPK - 1] > s
    return b8, b4, b2, b1, b0


def _peer_select(s1, s2, thr_s):
    nt8 = PEER_NKEYS // SUBLANES
    s1 = s1.reshape(nt8, SUBLANES, LANES)
    s2 = s2.reshape(nt8, SUBLANES, LANES)
    v1 = _top_sorted([s1[i] for i in range(nt8)])
    v2 = _top_sorted([s2[i] for i in range(nt8)])
    row8 = lax.broadcasted_iota(jnp.int32, (SUBLANES, LANES), 0)

    def column(v, start):
        col = v[start]
        for s in range(1, SUBLANES):
            col = jnp.where(row8 == s, v[start + s], col)
        return col

    v1_lo, v1_hi, v2_hi = column(v1, 0), column(v1, SUBLANES), column(v2, SUBLANES)
    cand = [v1_lo + v2[0], v1_hi + v2[0]]
    for r2, lim in enumerate(_CAND_LIMITS, start=1):
        g = v1_lo + v2[r2]
        cand.append(jnp.where(row8 < lim, g, NEG_INF) if lim < SUBLANES else g)
    cand.append(v2_hi + v1[0])
    tau = _top_sorted(cand, smallest_only=True)
    best = v1[0] + v2[0]
    sel = [c >= tau for c in cand]
    zt = jnp.zeros((SUBLANES, LANES), F32)
    for c, m in zip(cand, sel):
        zt = zt + jnp.where(m, jnp.exp(c - best), 0.0)
    z = jnp.sum(zt, axis=0, keepdims=True)
    self32 = [m.astype(F32) for m in sel]
    lo = self32[0]
    for t in self32[2:-1]:
        lo = lo + t
    tail = jnp.sum(self32[-1], axis=0, keepdims=True)
    counts = (lo + jnp.where(row8 == 0, tail, 0.0), self32[1])
    for half, cnt in enumerate(counts):
        thr = jnp.full((SUBLANES, LANES), POS_INF, F32)
        for k in range(PEER_TOPK):
            thr = jnp.where(cnt == k + 1.0, v2[k], thr)
        thr_s[half * SUBLANES:(half + 1) * SUBLANES, :] = thr
    bits1 = _rank_bits(v1, s1)
    level = [thr_s[r:r + 1, :] for r in range(PEER_TOPK)]
    for bit in bits1[3::-1]:
        level = [jnp.where(bit, level[2 * i + 1], level[2 * i]) for i in range(len(level) // 2)]
    thr1 = jnp.where(bits1[4], POS_INF, level[0])
    e2 = jnp.exp(s2 - v2[0])
    e1h = jnp.exp(s1 - v1[0]) * (0.5 / z)
    return e2, thr1, e1h


def _peer_kernel(x_ref, g_ref, wq_ref, sk_ref, u_ref, v_ref, o_ref,
                 hn_s, q_s, s_s, thr_s, e2_s, t1_s, e1_s, a0_s, a1_s, w_s, y_s,
                 *, tm, te, tc, kb, nt):
    j = pl.program_id(1)
    nlc = tm // LANES
    gi = te // PEER_NKEYS
    tiles = (PEER_NKEYS // SUBLANES, SUBLANES, LANES)

    @pl.when(j == 0)
    def _():
        hn = _rmsnorm(x_ref[...], g_ref[...]).astype(BF16)
        hn_s[...] = hn
        q_s[...] = _dot(hn, wq_ref[...]).astype(BF16)
        half = PEER_DKEY // 2
        for h in range(PEER_HEADS):
            for p in range(2):
                q_hp = q_s[:, (2 * h + p) * half:(2 * h + p + 1) * half]
                st = _dot_nt(sk_ref[h, p], q_hp)
                for ck in range(nlc):
                    s_s[2 * h + p, ck] = st[:, ck * LANES:(ck + 1) * LANES]

        def select(idx, carry):
            h = idx // nlc
            ck = idx % nlc
            e2, thr1, e1h = _peer_select(s_s[2 * h, ck], s_s[2 * h + 1, ck], thr_s)
            flat = (PEER_NKEYS, LANES)
            e2_s[h, ck] = e2.reshape(flat)
            t1_s[h, ck] = thr1.reshape(flat)
            e1_s[h, ck] = e1h.reshape(flat)
            return carry

        lax.fori_loop(0, PEER_HEADS * nlc, select, 0)
        y_s[...] = jnp.zeros_like(y_s)

    i1base = jnp.maximum(j - 1, 0) * gi
    per_kb = kb // PEER_NKEYS

    def step(a_w, a_r):
        for c in range(tm // tc if a_w is not None else 0):
            tok = slice(c * tc, (c + 1) * tc)
            a_w[c] = _dot_nt(u_ref[...], hn_s[tok, :])
        for b in range(te // kb if a_r is not None else 0):
            grp, l0 = divmod(b * per_kb, SUBLANES)
            i1s = pl.ds(pl.multiple_of(i1base + grp * SUBLANES, SUBLANES), SUBLANES)
            for ck in range(nlc):
                c, lc = divmod(ck, tc // LANES)
                ls = slice(lc * LANES, (lc + 1) * LANES)
                thr = [t1_s[h, ck, i1s, :] for h in range(PEER_HEADS)]
                e1 = [e1_s[h, ck, i1s, :] for h in range(PEER_HEADS)]
                rows = range(l0, l0 + per_kb)
                accs = [jnp.zeros(tiles, F32) for _ in rows]
                for h in range(PEER_HEADS):
                    s2 = s_s[2 * h + 1, ck].reshape(tiles)
                    e2 = e2_s[h, ck].reshape(tiles)
                    for n, l in enumerate(rows):
                        tb = jnp.broadcast_to(thr[h][l:l + 1], (SUBLANES, LANES))
                        eb = jnp.broadcast_to(e1[h][l:l + 1], (SUBLANES, LANES))
                        accs[n] = accs[n] + jnp.where(s2 >= tb, e2, 0.0) * eb
                for acc, l in zip(accs, rows):
                    rs = slice((grp * SUBLANES + l) * PEER_NKEYS, (grp * SUBLANES + l + 1) * PEER_NKEYS)
                    a = a_r[c, rs, ls].reshape(tiles)
                    gelu2 = a * (1.0 + lax.erf(a * math.sqrt(0.5)))
                    w = (acc * gelu2).reshape(PEER_NKEYS, LANES).astype(BF16)
                    w_s[ck * LANES:(ck + 1) * LANES, rs] = w.T
            es = slice(b * kb, (b + 1) * kb)
            y_s[...] += _dot(w_s[:, es], v_ref[es, :])

    inner = (j > 0) & (j < nt)

    @pl.when(j == 0)
    def _():
        step(a0_s, None)

    @pl.when(inner & (j % 2 == 0))
    def _():
        step(a0_s, a1_s)

    @pl.when(inner & (j % 2 == 1))
    def _():
        step(a1_s, a0_s)

    @pl.when(j == nt)
    def _():
        step(None, a1_s if nt % 2 == 0 else a0_s)
        o_ref[...] = x_ref[...] + y_s[...]


def _peer(x2, g, wq, sk, u_tab, v_tab, layer, tm=512, te=2048, tc=256, kb=256):
    n = x2.shape[0]
    nt = u_tab.shape[1] // te
    nlc = tm // LANES
    return pl.pallas_call(
        functools.partial(_peer_kernel, tm=tm, te=te, tc=tc, kb=kb, nt=nt),
        grid=(n // tm, nt + 1),
        in_specs=[
            pl.BlockSpec((tm, D_MODEL), lambda i, j: (i, 0)),
            pl.BlockSpec((1, D_MODEL), lambda i, j: (0, 0)),
            pl.BlockSpec((None,) + wq.shape[1:], lambda i, j: (layer, 0, 0), pipeline_mode=pl.Buffered(1)),
            pl.BlockSpec((None,) + sk.shape[1:], lambda i, j: (layer, 0, 0, 0, 0), pipeline_mode=pl.Buffered(1)),
            pl.BlockSpec((None, te, D_MODEL), lambda i, j: (layer, jnp.minimum(j, nt - 1), 0)),
            pl.BlockSpec((None, te, D_MODEL), lambda i, j: (layer, jnp.maximum(j - 1, 0), 0)),
        ],
        out_specs=pl.BlockSpec((tm, D_MODEL), lambda i, j: (i, 0)),
        out_shape=jax.ShapeDtypeStruct((n, D_MODEL), F32),
        scratch_shapes=[
            pltpu.VMEM((tm, D_MODEL), BF16),
            pltpu.VMEM((tm, PEER_HEADS * PEER_DKEY), BF16),
            pltpu.VMEM((2 * PEER_HEADS, nlc, PEER_NKEYS, LANES), F32),
            pltpu.VMEM((PEER_TOPK, LANES), F32),
            pltpu.VMEM((PEER_HEADS, nlc, PEER_NKEYS, LANES), F32),
            pltpu.VMEM((PEER_HEADS, nlc, PEER_NKEYS, LANES), F32),
            pltpu.VMEM((PEER_HEADS, nlc, PEER_NKEYS, LANES), F32),
            pltpu.VMEM((tm // tc, te, tc), F32),
            pltpu.VMEM((tm // tc, te, tc), F32),
            pltpu.VMEM((tm, te), BF16),
            pltpu.VMEM((tm, D_MODEL), F32),
        ],
        compiler_params=_cparams(2),
        name="peer",
    )(x2, g, wq, sk, u_tab, v_tab)


def _rope_tables(pos):
    half = HEAD_DIM // 2
    inv_freq = jnp.power(jnp.float32(ROPE_THETA), -jnp.arange(half, dtype=F32) * (2.0 / HEAD_DIM))
    ang = pos.astype(F32)[:, None] * inv_freq[None, :]
    cos = jnp.concatenate([jnp.cos(ang)] * 4, axis=1)
    sin = jnp.concatenate([jnp.sin(ang)] * 4, axis=1)
    return cos, sin


def kernel(x_prompt, x_sample, state_pool, cache_k, cache_v, norm_mix, norm_ffn, pool_w_in, pool_w_group,
           pool_scale, attn_w_qkv, attn_q_norm, attn_k_norm, attn_sinks, attn_w_o, peer_w_q, peer_subkeys,
           peer_u, peer_v):
    b, s, _ = x_prompt.shape
    db, t, _ = x_sample.shape
    depth = norm_mix.shape[0]
    wq_all = peer_w_q.astype(BF16)
    sk_all = peer_subkeys.astype(BF16)
    u_all = peer_u.astype(BF16)
    v_all = peer_v.astype(BF16)
    xp = x_prompt
    xs = x_sample.reshape(db * t, D_MODEL)
    seg = jnp.asarray(np.kron(np.eye(256 // HEAD_DIM), np.ones((HEAD_DIM, HEAD_DIM))), BF16)
    cos_p, sin_p = _rope_tables(jnp.arange(s, dtype=jnp.int32))
    cos_s, sin_s = _rope_tables(PAST_LEN + jnp.arange(t, dtype=jnp.int32))
    pool_p, pool_s, kp_l, vp_l, ks_l, vs_l = [], [], [], [], [], []
    for i in range(depth):
        jl = i // 2
        g_mix = norm_mix[i][None, :]
        if i % 2 == 0:
            win = pool_w_in[jl].astype(BF16)
            wg = pool_w_group[jl].astype(BF16)
            sc = pool_scale[jl][None, :]
            xp, st_p = _pool_prompt(xp, g_mix, win, wg, sc)
            prior16 = jnp.pad(state_pool[jl], ((0, 0), (1, 0), (0, 0)))
            xs, st_s = _pool_sample(xs, prior16, g_mix, win, wg, sc, t)
            pool_p.append(st_p[:, 1:])
            pool_s.append(st_s[:, 1:])
        else:
            wqkv = attn_w_qkv[jl].astype(BF16)
            wo = attn_w_o[jl].astype(BF16)
            gain = jnp.concatenate([jnp.tile(attn_q_norm[jl], N_HEADS), jnp.tile(attn_k_norm[jl], N_KV_HEADS)])[None, :]
            sinks = attn_sinks[jl]
            tq = 512
            qp, kp, vp = _qkv(xp.reshape(b * s, D_MODEL), g_mix, wqkv, seg, gain, cos_p, sin_p, tq, s // tq)
            kp3 = kp.reshape(b, s, KV_WIDTH)
            vp3 = vp.reshape(b, s, KV_WIDTH)
            xp = _attn_prompt(xp, qp.reshape(b, s, Q_WIDTH), kp3, vp3, sinks, wo)
            cos_st = jnp.tile(cos_s, (tq // t, 1))
            sin_st = jnp.tile(sin_s, (tq // t, 1))
            qs, kq, vq = _qkv(xs, g_mix, wqkv, seg, gain, cos_st, sin_st, tq, 1)
            ck = cache_k[jl].reshape(db, WINDOW, KV_WIDTH)
            cv = cache_v[jl].reshape(db, WINDOW, KV_WIDTH)
            xs, nk_s, nv_s = _attn_sample(xs, qs, kq, vq, ck, cv, sinks, wo, t)
            kp_l.append(kp3[:, -WINDOW:].reshape(b, WINDOW, N_KV_HEADS, HEAD_DIM))
            vp_l.append(vp3[:, -WINDOW:].reshape(b, WINDOW, N_KV_HEADS, HEAD_DIM))
            ks_l.append(nk_s.reshape(db, WINDOW, N_KV_HEADS, HEAD_DIM))
            vs_l.append(nv_s.reshape(db, WINDOW, N_KV_HEADS, HEAD_DIM))
        g_ffn = norm_ffn[i][None, :]
        xp = _peer(xp.reshape(b * s, D_MODEL), g_ffn, wq_all, sk_all, u_all, v_all, i).reshape(b, s, D_MODEL)
        xs = _peer(xs, g_ffn, wq_all, sk_all, u_all, v_all, i)
    return (xp, xs.reshape(db, t, D_MODEL), jnp.stack(pool_p), jnp.stack(pool_s), jnp.stack(kp_l),
            jnp.stack(vp_l), jnp.stack(ks_l), jnp.stack(vs_l))
```
